```python
import jax, jax.numpy as jnp
from jax import lax
import numpy as np

D_MODEL = 2048
BATCH = 4
SEQ = 4096
DEPTH = 2

CHUNK = 64
N_MIXERS = 2
EPS = 1e-6

SSM_EXPAND = 2
D_INNER = SSM_EXPAND * D_MODEL
SSM_HEADDIM = 64
SSM_HEADS = D_INNER // SSM_HEADDIM
SSM_GROUPS = 8
SSM_HEADS_PER_GROUP = SSM_HEADS // SSM_GROUPS
SSM_STATE = 128
SSM_CONV = 4
SSD_CHUNK = CHUNK
CONV_DIM = D_INNER + 2 * SSM_GROUPS * SSM_STATE
SSM_IN = D_INNER + CONV_DIM + SSM_HEADS
DT_MIN = 1e-3
DT_MAX = 1e-1

POOL_EXPAND = 2
D_POOL = POOL_EXPAND * D_MODEL
POOL_WINDOWS = (2, 4, 8, 16)
POOL_GROUPS = len(POOL_WINDOWS)
POOL_GROUP_DIM = D_POOL // POOL_GROUPS

kernel_name = "hybrid_ssd_multiscale_pool_trunk"


def rmsnorm(x, g):
    xf = x.astype(jnp.float32)
    y = xf * lax.rsqrt(jnp.mean(xf * xf, axis=-1, keepdims=True) + EPS)
    return (y * g.astype(jnp.float32)).astype(x.dtype)


def causal_depthwise_conv(x, w, b):
    k_taps = w.shape[0]
    length = x.shape[1]
    xp = jnp.pad(x, ((0, 0), (k_taps - 1, 0), (0, 0)))
    y = b
    for k in range(k_taps):
        y = y + xp[:, k:k + length] * w[k]
    return y


def ssd_chunked_scan(xdt, a_dt, bm, cm):
    bsz, length, g, r, p = xdt.shape
    n = bm.shape[-1]
    n_chunks = length // SSD_CHUNK

    def to_chunks(t):
        t = t.reshape((bsz, n_chunks, SSD_CHUNK) + t.shape[2:])
        return jnp.moveaxis(t, 1, 0)

    mask = jnp.tril(jnp.ones((SSD_CHUNK, SSD_CHUNK), dtype=bool))[None, :, :, None, None]

    def step(state, inp):
        xc, ac, bc, cc = inp
        cs = jnp.cumsum(ac, axis=1)
        seg = cs[:, :, None] - cs[:, None, :]
        decay = jnp.exp(jnp.where(mask, seg, -jnp.inf))
        cb = jnp.einsum("blgn,bsgn->blsg", cc, bc)
        y_diag = jnp.einsum("blsg,blsgr,bsgrp->blgrp", cb, decay, xc)
        y_off = jnp.einsum("blgn,bgrpn,blgr->blgrp", cc, state, jnp.exp(cs))
        last = cs[:, -1]
        w_in = jnp.exp(last[:, None] - cs)
        new_state = state * jnp.exp(last)[..., None, None] + jnp.einsum(
            "bsgn,bsgr,bsgrp->bgrpn", bc, w_in, xc)
        return new_state, y_diag + y_off

    state0 = jnp.zeros((bsz, g, r, p, n), jnp.float32)
    _, ys = lax.scan(step, state0, (to_chunks(xdt), to_chunks(a_dt), to_chunks(bm), to_chunks(cm)))
    return jnp.moveaxis(ys, 0, 1).reshape(bsz, length, g, r, p)


def ssm_mixer(h, w_in, conv_w, conv_b, dt_bias, a_log, d_skip, norm_g, w_out):
    bsz, length, _ = h.shape
    g, r, p, n = SSM_GROUPS, SSM_HEADS_PER_GROUP, SSM_HEADDIM, SSM_STATE
    proj = h @ w_in
    z = proj[..., :D_INNER]
    xbc = proj[..., D_INNER:D_INNER + CONV_DIM]
    dt = proj[..., D_INNER + CONV_DIM:]
    xbc = jax.nn.silu(causal_depthwise_conv(xbc, conv_w, conv_b)).astype(jnp.float32)
    xs = xbc[..., :D_INNER].reshape(bsz, length, g, r, p)
    bm = xbc[..., D_INNER:D_INNER + g * n].reshape(bsz, length, g, n)
    cm = xbc[..., D_INNER + g * n:].reshape(bsz, length, g, n)
    dt = jax.nn.softplus(dt.astype(jnp.float32) + dt_bias.astype(jnp.float32))
    dt = dt.reshape(bsz, length, g, r)
    a = -jnp.exp(a_log.astype(jnp.float32)).reshape(g, r)
    y = ssd_chunked_scan(xs * dt[..., None], dt * a, bm, cm)
    y = y + d_skip.astype(jnp.float32).reshape(g, r, 1) * xs
    y = y.reshape(bsz, length, D_INNER) * jax.nn.silu(z.astype(jnp.float32))
    y = rmsnorm(y, norm_g)
    return y.astype(h.dtype) @ w_out


def pool_mixer(h, w_in, w_group, scale, w_out):
    bsz, length, _ = h.shape
    proj = h @ w_in
    u = proj[..., :D_POOL].astype(jnp.float32).reshape(bsz, length, POOL_GROUPS, POOL_GROUP_DIM)
    gate = proj[..., D_POOL:].astype(jnp.float32)
    cs = jnp.cumsum(u, axis=1)
    pos = jnp.arange(1, length + 1, dtype=jnp.int32)
    means = []
    for gi, win in enumerate(POOL_WINDOWS):
        c = cs[:, :, gi]
        shifted = jnp.pad(c, ((0, 0), (win, 0), (0, 0)))[:, :length]
        cnt = jnp.minimum(pos, win).astype(jnp.float32)[None, :, None]
        means.append((c - shifted) / cnt)
    mixed = jnp.stack(means, axis=2) - u
    mixed = jnp.einsum("blgc,gcd->blgd", mixed, w_group.astype(jnp.float32))
    mixed = mixed.reshape(bsz, length, D_POOL) * scale.astype(jnp.float32)
    y = mixed * jax.nn.silu(gate)
    return y.astype(h.dtype) @ w_out


def setup_inputs(seed: int = 0) -> dict:
    key = jax.random.key(seed)
    ks = jax.random.split(key, 20)
    n_a = (DEPTH + N_MIXERS - 1) // N_MIXERS
    n_b = DEPTH // N_MIXERS
    f32 = jnp.float32
    x = jax.random.normal(ks[0], (BATCH, SEQ, D_MODEL), f32)
    ln_g = 1.0 + 0.05 * jax.random.normal(ks[1], (DEPTH, D_MODEL), f32)
    final_g = 1.0 + 0.05 * jax.random.normal(ks[2], (D_MODEL,), f32)
    ssm_w_in = jax.random.normal(ks[3], (n_a, D_MODEL, SSM_IN), f32) * D_MODEL ** -0.5
    ssm_conv_w = jax.random.normal(ks[4], (n_a, SSM_CONV, CONV_DIM), f32) * SSM_CONV ** -0.5
    ssm_conv_b = 0.01 * jax.random.normal(ks[5], (n_a, CONV_DIM), f32)
    u_dt = jax.random.uniform(ks[6], (n_a, SSM_HEADS), f32)
    dt0 = jnp.exp(u_dt * (np.log(DT_MAX) - np.log(DT_MIN)) + np.log(DT_MIN))
    ssm_dt_bias = dt0 + jnp.log(-jnp.expm1(-dt0))
    ssm_a_log = jnp.log(jax.random.uniform(ks[7], (n_a, SSM_HEADS), f32, 1.0, 16.0))
    ssm_d = 1.0 + 0.1 * jax.random.normal(ks[8], (n_a, SSM_HEADS), f32)
    ssm_norm_g = 1.0 + 0.05 * jax.random.normal(ks[9], (n_a, D_INNER), f32)
    ssm_w_out = jax.random.normal(ks[10], (n_a, D_INNER, D_MODEL), f32) * D_INNER ** -0.5
    pool_w_in = jax.random.normal(ks[11], (n_b, D_MODEL, 2 * D_POOL), f32) * D_MODEL ** -0.5
    pool_w_group = jax.random.normal(ks[12], (n_b, POOL_GROUPS, POOL_GROUP_DIM, POOL_GROUP_DIM), f32) * POOL_GROUP_DIM ** -0.5
    pool_scale = 1.0 + 0.1 * jax.random.normal(ks[13], (n_b, D_POOL), f32)
    pool_w_out = jax.random.normal(ks[14], (n_b, D_POOL, D_MODEL), f32) * D_POOL ** -0.5
    return {"x": x, "ln_g": ln_g, "final_g": final_g,
            "ssm_w_in": ssm_w_in, "ssm_conv_w": ssm_conv_w, "ssm_conv_b": ssm_conv_b,
            "ssm_dt_bias": ssm_dt_bias, "ssm_a_log": ssm_a_log, "ssm_d": ssm_d,
            "ssm_norm_g": ssm_norm_g, "ssm_w_out": ssm_w_out,
            "pool_w_in": pool_w_in, "pool_w_group": pool_w_group,
            "pool_scale": pool_scale, "pool_w_out": pool_w_out}


def reference(x, ln_g, final_g, ssm_w_in, ssm_conv_w, ssm_conv_b, ssm_dt_bias, ssm_a_log,
              ssm_d, ssm_norm_g, ssm_w_out, pool_w_in, pool_w_group, pool_scale, pool_w_out):
    for i in range(DEPTH):
        j = i // N_MIXERS
        hn = rmsnorm(x, ln_g[i])
        if i % N_MIXERS == 0:
            x = x + ssm_mixer(hn, ssm_w_in[j], ssm_conv_w[j], ssm_conv_b[j], ssm_dt_bias[j],
                              ssm_a_log[j], ssm_d[j], ssm_norm_g[j], ssm_w_out[j])
        else:
            x = x + pool_mixer(hn, pool_w_in[j], pool_w_group[j], pool_scale[j], pool_w_out[j])
    return rmsnorm(x, final_g)
```

```python
import functools

import jax
import jax.numpy as jnp
from jax import lax
from jax.experimental import pallas as pl
from jax.experimental.pallas import tpu as pltpu

F32 = jnp.float32
BF16 = jnp.bfloat16

EPS = 1e-6
LANES = 128
SUBLANES = 8
V7X_VMEM_BYTES = 64 * 1024 * 1024

SSD_CHUNK = 64
HEADDIM = 64
N_STATE = 128
N_GROUPS = 8
HEADS_PER_GROUP = 8
GROUP_WIDTH = HEADS_PER_GROUP * HEADDIM
CONV_TAPS = 4
POOL_WINDOWS = (2, 4, 8, 16)
POOL_HALO = 16
CHUNK_PAIR = 2 * SSD_CHUNK


def _vmem_limit(block_bytes):
    return int(min(block_bytes + 20 * 1024 * 1024, V7X_VMEM_BYTES - 4 * 1024 * 1024))


def _rmsnorm(x, g):
    ms = jnp.mean(x * x, axis=-1, keepdims=True)
    return x * lax.rsqrt(ms + EPS) * g


def _silu(x):
    return x * jax.nn.sigmoid(x)


def _softplus(x):
    return jnp.maximum(x, 0.0) + jnp.log1p(jnp.exp(-jnp.abs(x)))


def _ssm_inproj_kernel(x_ref, g_ref, w_ref, wdt_ref, cw_ref, cb_ref, dtb_ref,
                       out_ref, dt_ref, hn_ref, ext_ref, carry_ref,
                       *, n_z_tiles, tiles_per_seq):
    i = pl.program_id(0)
    j = pl.program_id(1)
    tm = out_ref.shape[0]

    @pl.when(j == 0)
    def _():
        hn_ref[...] = _rmsnorm(x_ref[...], g_ref[...]).astype(BF16)

    acc = jnp.dot(hn_ref[...], w_ref[...], preferred_element_type=F32)

    @pl.when(j < n_z_tiles)
    def _():
        out_ref[...] = acc

    @pl.when(j >= n_z_tiles)
    def _():
        jc = j - n_z_tiles

        @pl.when(i % tiles_per_seq == 0)
        def _():
            carry_ref[jc] = jnp.zeros(carry_ref.shape[1:], F32)

        ext_ref[0:SUBLANES, :] = carry_ref[jc]
        ext_ref[SUBLANES:, :] = acc
        carry_ref[jc] = ext_ref[tm:tm + SUBLANES, :]
        cw = cw_ref[...]
        y = cb_ref[...]
        for k in range(CONV_TAPS):
            off = SUBLANES - (CONV_TAPS - 1) + k
            y = y + ext_ref[off:off + tm, :] * cw[k:k + 1, :]
        out_ref[...] = _silu(y)

    @pl.when(j == pl.num_programs(1) - 1)
    def _():
        d = jnp.dot(hn_ref[...], wdt_ref[...], preferred_element_type=F32)
        dt_ref[...] = _softplus(d + dtb_ref[...])


def _ssm_inproj(x2, g, w, w_dt, conv_w, conv_b, dt_bias, *, seq_len, d_inner, conv_dim, tm=1024, tn=512):
    m, d = x2.shape
    n_main = d_inner + conv_dim
    n_z_tiles = d_inner // tn
    n_tiles = n_main // tn
    dt_w = w_dt.shape[1]
    blocks = (2 * tm * d * 4 + tm * d * 2 + 2 * d * tn * 2 + 2 * d * dt_w * 2 + 2 * tm * tn * 4
              + 2 * tm * dt_w * 4 + (tm + SUBLANES) * tn * 4)
    conv_idx = lambda i, j: (0, jnp.maximum(j - n_z_tiles, 0))
    return pl.pallas_call(
        functools.partial(_ssm_inproj_kernel, n_z_tiles=n_z_tiles, tiles_per_seq=seq_len // tm),
        grid=(m // tm, n_tiles),
        in_specs=[
            pl.BlockSpec((tm, d), lambda i, j: (i, 0)),
            pl.BlockSpec((1, d), lambda i, j: (0, 0)),
            pl.BlockSpec((d, tn), lambda i, j: (0, j)),
            pl.BlockSpec((d, dt_w), lambda i, j: (0, 0)),
            pl.BlockSpec((CONV_TAPS, tn), conv_idx),
            pl.BlockSpec((1, tn), conv_idx),
            pl.BlockSpec((1, dt_w), lambda i, j: (0, 0)),
        ],
        out_specs=[
            pl.BlockSpec((tm, tn), lambda i, j: (i, j)),
            pl.BlockSpec((tm, dt_w), lambda i, j: (i, 0)),
        ],
        out_shape=[
            jax.ShapeDtypeStruct((m, n_main), F32),
            jax.ShapeDtypeStruct((m, dt_w), F32),
        ],
        scratch_shapes=[
            pltpu.VMEM((tm, d), BF16),
            pltpu.VMEM((tm + SUBLANES, tn), F32),
            pltpu.VMEM((conv_dim // tn, SUBLANES, tn), F32),
        ],
        compiler_params=pltpu.CompilerParams(
            dimension_semantics=("arbitrary", "arbitrary"),
            vmem_limit_bytes=_vmem_limit(blocks)),
        name="ssm_inproj",
    )(x2, g, w, w_dt, conv_w, conv_b, dt_bias)


def _split3(x):
    hi = x.astype(BF16)
    r1 = x - hi.astype(F32)
    mid = r1.astype(BF16)
    lo = (r1 - mid.astype(F32)).astype(BF16)
    return hi, mid, lo


def _ssd_kernel(z_ref, xs_ref, bc_ref, dt_ref, alog_ref, dskip_ref, ng_ref,
                out_ref, state_ref, y_ref):
    tb = z_ref.shape[0]
    q = SSD_CHUNK

    @pl.when(pl.program_id(1) == 0)
    def _():
        state_ref[...] = jnp.zeros_like(state_ref)

    a = -jnp.exp(alog_ref[...])

    r2 = lax.broadcasted_iota(jnp.int32, (CHUNK_PAIR, CHUNK_PAIR), 0)
    c2 = lax.broadcasted_iota(jnp.int32, (CHUNK_PAIR, CHUNK_PAIR), 1)
    tril_pair = jnp.where((r2 // q == c2 // q) & (r2 >= c2), 1.0, 0.0).astype(BF16)

    lrow = lax.broadcasted_iota(jnp.int32, (q, LANES), 0)
    lane = lax.broadcasted_iota(jnp.int32, (q, LANES), 1)
    lane_hi = lane >= q
    lane_hi_i = lane_hi.astype(jnp.int32)
    causal_pair = lrow >= (lane % q)
    lane_hi_row = lax.broadcasted_iota(jnp.int32, (1, LANES), 1) >= q
    head_of_lane = lax.broadcasted_iota(jnp.int32, (q, 4 * HEADDIM), 1) // HEADDIM

    def expand_heads(v, h0):
        idx = lane_hi_i[:v.shape[0]]
        parts = [jnp.take_along_axis(v, idx + (h0 + 2 * p), axis=1, mode="promise_in_bounds")
                 for p in range(HEADS_PER_GROUP // 2)]
        return jnp.concatenate(parts, axis=1)

    def pair_body(p, carry):
        r0 = pl.multiple_of(p * CHUNK_PAIR, CHUNK_PAIR)
        dt2 = dt_ref[pl.ds(r0, CHUNK_PAIR), :]
        hi, mid, lo = _split3(dt2 * a)
        cs = (jnp.dot(tril_pair, hi, preferred_element_type=F32)
              + jnp.dot(tril_pair, mid, preferred_element_type=F32)
              + jnp.dot(tril_pair, lo, preferred_element_type=F32))
        cs_t = cs.T
        dt_t = dt2.T
        cs_t_rot = pltpu.roll(cs_t, q, 1)
        dt_t_rot = pltpu.roll(dt_t, q, 1)

        for c in range(2):
            rows = pl.ds(r0 + c * q, q)
            cs_c = cs[c * q:(c + 1) * q]
            dt_c = dt2[c * q:(c + 1) * q]
            last = cs_c[q - 1:q, :]
            ecs = jnp.exp(cs_c)
            wdt = jnp.exp(last - cs_c) * dt_c
            last8 = cs_c[q - SUBLANES:q, :]
            cs_lo, cs_hi = (cs_t, cs_t_rot) if c == 0 else (cs_t_rot, cs_t)
            dt_lo, dt_hi = (dt_t, dt_t_rot) if c == 0 else (dt_t_rot, dt_t)

            for g in range(N_GROUPS):
                h0 = g * HEADS_PER_GROUP
                xs = xs_ref[rows, g * GROUP_WIDTH:(g + 1) * GROUP_WIDTH]
                bm = bc_ref[rows, g * N_STATE:(g + 1) * N_STATE].astype(BF16)
                cm = bc_ref[rows, (N_GROUPS + g) * N_STATE:(N_GROUPS + g + 1) * N_STATE].astype(BF16)
                cb2 = lax.dot_general(cm, jnp.concatenate([bm, bm], axis=0),
                                      (((1,), (1,)), ((), ())), preferred_element_type=F32)
                lhs_parts = []
                for p2 in range(HEADS_PER_GROUP // 2):
                    ha = h0 + 2 * p2
                    col = jnp.take_along_axis(cs_c, lane_hi_i + ha, axis=1, mode="promise_in_bounds")
                    row = jnp.where(lane_hi_row, cs_hi[ha + 1:ha + 2, :], cs_lo[ha:ha + 1, :])
                    dtrow = jnp.where(lane_hi_row, dt_hi[ha + 1:ha + 2, :], dt_lo[ha:ha + 1, :])
                    decay = jnp.exp(jnp.where(causal_pair, col - row, -jnp.inf))
                    lhs_parts.append((cb2 * decay * dtrow).astype(BF16))
                ydiag = []
                for k in range(2):
                    lhs = jnp.concatenate(lhs_parts[2 * k:2 * k + 2], axis=1)
                    xh = xs[:, k * 4 * HEADDIM:(k + 1) * 4 * HEADDIM]
                    rhs = jnp.concatenate(
                        [jnp.where(head_of_lane == r, xh, 0.0) for r in range(4)], axis=0).astype(BF16)
                    ydiag.append(jnp.dot(lhs, rhs, preferred_element_type=F32))
                s_g = state_ref[g]
                yoff = jnp.dot(cm, s_g.astype(BF16), preferred_element_type=F32)
                y = jnp.concatenate(ydiag, axis=1) + yoff * expand_heads(ecs, h0)
                y = y + dskip_ref[:, g * GROUP_WIDTH:(g + 1) * GROUP_WIDTH] * xs
                y_ref[rows, g * GROUP_WIDTH:(g + 1) * GROUP_WIDTH] = y
                xw = (xs * expand_heads(wdt, h0)).astype(BF16)
                upd = lax.dot_general(bm, xw, (((0,), (0,)), ((), ())), preferred_element_type=F32)
                elast = jnp.exp(expand_heads(last8, h0)[SUBLANES - 1:SUBLANES, :])
                state_ref[g] = s_g * elast + upd
        return carry

    lax.fori_loop(0, tb // CHUNK_PAIR, pair_body, 0)

    slab = 2 * SUBLANES

    def gate_body(s, carry):
        rr = pl.ds(pl.multiple_of(s * slab, slab), slab)
        yg = y_ref[rr, :] * _silu(z_ref[rr, :])
        out_ref[rr, :] = _rmsnorm(yg, ng_ref[...]).astype(BF16)
        return carry

    lax.fori_loop(0, tb // slab, gate_body, 0)


def _ssd_scan(proj, dt, a_log, d_skip, norm_g, *, batch, seq_len, d_inner, tb=256):
    m = proj.shape[0]
    bc_w = 2 * N_GROUPS * N_STATE
    n_l = seq_len // tb
    row = lambda b, l: b * n_l + l
    blocks = (2 * 2 * tb * d_inner * 4 + 2 * tb * bc_w * 4 + 2 * tb * LANES * 4 + 2 * tb * d_inner * 2
              + N_GROUPS * N_STATE * GROUP_WIDTH * 4 + tb * d_inner * 4)
    return pl.pallas_call(
        _ssd_kernel,
        grid=(batch, n_l),
        in_specs=[
            pl.BlockSpec((tb, d_inner), lambda b, l: (row(b, l), 0)),
            pl.BlockSpec((tb, d_inner), lambda b, l: (row(b, l), 1)),
            pl.BlockSpec((tb, bc_w), lambda b, l: (row(b, l), 2 * d_inner // bc_w)),
            pl.BlockSpec((tb, LANES), lambda b, l: (row(b, l), 0)),
            pl.BlockSpec((1, LANES), lambda b, l: (0, 0)),
            pl.BlockSpec((1, d_inner), lambda b, l: (0, 0)),
            pl.BlockSpec((1, d_inner), lambda b, l: (0, 0)),
        ],
        out_specs=pl.BlockSpec((tb, d_inner), lambda b, l: (row(b, l), 0)),
        out_shape=jax.ShapeDtypeStruct((m, d_inner), BF16),
        scratch_shapes=[
            pltpu.VMEM((N_GROUPS, N_STATE, GROUP_WIDTH), F32),
            pltpu.VMEM((tb, d_inner), F32),
        ],
        compiler_params=pltpu.CompilerParams(
            dimension_semantics=("arbitrary", "arbitrary"),
            vmem_limit_bytes=_vmem_limit(blocks)),
        name="ssd_scan",
    )(proj, proj, proj, dt, a_log, d_skip, norm_g)


def _outproj_kernel(a_ref, w_ref, x_ref, g_ref, *out_refs, final):
    acc = jnp.dot(a_ref[...], w_ref[...], preferred_element_type=F32)
    x1 = x_ref[...] + acc
    hn = _rmsnorm(x1, g_ref[...])
    if final:
        out_refs[0][...] = hn
    else:
        out_refs[0][...] = x1
        out_refs[1][...] = hn.astype(BF16)


def _outproj(a, w, x2, g, *, final, tm=512):
    m, k = a.shape
    d = w.shape[1]
    row_spec = pl.BlockSpec((tm, d), lambda i: (i, 0))
    if final:
        out_specs = [row_spec]
        out_shape = [jax.ShapeDtypeStruct((m, d), F32)]
    else:
        out_specs = [row_spec, row_spec]
        out_shape = [jax.ShapeDtypeStruct((m, d), F32), jax.ShapeDtypeStruct((m, d), BF16)]
    blocks = 2 * tm * k * 2 + k * d * 2 + 2 * tm * d * 4 + 2 * tm * d * 4 + 2 * tm * d * 2
    return pl.pallas_call(
        functools.partial(_outproj_kernel, final=final),
        grid=(m // tm,),
        in_specs=[
            pl.BlockSpec((tm, k), lambda i: (i, 0)),
            pl.BlockSpec((k, d), lambda i: (0, 0), pipeline_mode=pl.Buffered(1)),
            row_spec,
            pl.BlockSpec((1, d), lambda i: (0, 0)),
        ],
        out_specs=out_specs,
        out_shape=out_shape,
        compiler_params=pltpu.CompilerParams(
            dimension_semantics=("arbitrary",),
            vmem_limit_bytes=_vmem_limit(blocks)),
        name="outproj_final" if final else "outproj",
    )(a, w, x2, g)


def _pool_inproj_kernel(h_ref, w_ref, out_ref, *scratch, pooled, tiles_per_seq, tiles_per_group):
    acc = jnp.dot(h_ref[...], w_ref[...], preferred_element_type=F32)
    if not pooled:
        out_ref[...] = acc
        return
    ext_ref, carry_ref = scratch
    i = pl.program_id(0)
    j = pl.program_id(1)
    tm, tn = out_ref.shape

    @pl.when(i % tiles_per_seq == 0)
    def _():
        carry_ref[j] = jnp.zeros(carry_ref.shape[1:], F32)

    ext_ref[0:POOL_HALO, :] = carry_ref[j]
    ext_ref[POOL_HALO:, :] = acc
    carry_ref[j] = ext_ref[tm:tm + POOL_HALO, :]
    pos = (i % tiles_per_seq) * tm + lax.broadcasted_iota(jnp.int32, (tm, tn), 0) + 1

    for k, win in enumerate(POOL_WINDOWS):
        @pl.when(j // tiles_per_group == k)
        def _(win=win):
            s = ext_ref[POOL_HALO:POOL_HALO + tm, :]
            for d in range(1, win):
                s = s + ext_ref[POOL_HALO - d:POOL_HALO - d + tm, :]
            cnt = jnp.minimum(pos, win).astype(F32)
            out_ref[...] = (s / cnt - ext_ref[POOL_HALO:POOL_HALO + tm, :]).astype(out_ref.dtype)


def _pool_inproj(h, w, *, col0, pooled, seq_len, d_pool, tm=1024, tn=512):
    m, d = h.shape
    n_tiles = d_pool // tn
    j0 = col0 // tn
    scratch = []
    if pooled:
        scratch = [pltpu.VMEM((tm + POOL_HALO, tn), F32), pltpu.VMEM((n_tiles, POOL_HALO, tn), F32)]
    out_dtype = BF16 if pooled else F32
    blocks = 2 * tm * d * 2 + 2 * d * tn * 2 + 2 * tm * tn * 4 + (tm + POOL_HALO) * tn * 4
    return pl.pallas_call(
        functools.partial(_pool_inproj_kernel, pooled=pooled, tiles_per_seq=seq_len // tm,
                          tiles_per_group=d_pool // len(POOL_WINDOWS) // tn),
        grid=(m // tm, n_tiles),
        in_specs=[
            pl.BlockSpec((tm, d), lambda i, j: (i, 0)),
            pl.BlockSpec((d, tn), lambda i, j: (0, j0 + j)),
        ],
        out_specs=pl.BlockSpec((tm, tn), lambda i, j: (i, j)),
        out_shape=jax.ShapeDtypeStruct((m, d_pool), out_dtype),
        scratch_shapes=scratch,
        compiler_params=pltpu.CompilerParams(
            dimension_semantics=("arbitrary", "arbitrary"),
            vmem_limit_bytes=_vmem_limit(blocks)),
        name="pool_inproj_u" if pooled else "pool_inproj_gate",
    )(h, w)


def _pool_group_kernel(m_ref, w_ref, gate_ref, scale_ref, out_ref):
    t = jnp.dot(m_ref[...], w_ref[...], preferred_element_type=F32)
    out_ref[...] = (t * scale_ref[...] * _silu(gate_ref[...])).astype(out_ref.dtype)


def _pool_group(mixed, w_group, gate, scale, *, tm=1024):
    m, d_pool = mixed.shape
    n_groups, gd, _ = w_group.shape
    blocks = 2 * tm * gd * 2 + 2 * gd * gd * 2 + 2 * tm * gd * 4 + 2 * tm * gd * 2
    return pl.pallas_call(
        _pool_group_kernel,
        grid=(n_groups, m // tm),
        in_specs=[
            pl.BlockSpec((tm, gd), lambda g, i: (i, g)),
            pl.BlockSpec((None, gd, gd), lambda g, i: (g, 0, 0)),
            pl.BlockSpec((tm, gd), lambda g, i: (i, g)),
            pl.BlockSpec((1, gd), lambda g, i: (0, g)),
        ],
        out_specs=pl.BlockSpec((tm, gd), lambda g, i: (i, g)),
        out_shape=jax.ShapeDtypeStruct((m, d_pool), BF16),
        compiler_params=pltpu.CompilerParams(
            dimension_semantics=("arbitrary", "arbitrary"),
            vmem_limit_bytes=_vmem_limit(blocks)),
        name="pool_group",
    )(mixed, w_group, gate, scale)


def kernel(x, ln_g, final_g, ssm_w_in, ssm_conv_w, ssm_conv_b, ssm_dt_bias, ssm_a_log, ssm_d, ssm_norm_g, ssm_w_out, pool_w_in, pool_w_group, pool_scale, pool_w_out):
    batch, seq_len, d_model = x.shape
    m = batch * seq_len
    d_inner = ssm_w_out.shape[1]
    conv_dim = ssm_conv_w.shape[2]
    n_heads = ssm_dt_bias.shape[1]
    d_pool = pool_w_out.shape[1]
    assert d_inner == N_GROUPS * GROUP_WIDTH and conv_dim == d_inner + 2 * N_GROUPS * N_STATE
    assert n_heads == N_GROUPS * HEADS_PER_GROUP and n_heads <= LANES
    assert ssm_conv_w.shape[1] == CONV_TAPS and pool_w_group.shape[1] == len(POOL_WINDOWS)

    x2 = x.reshape(m, d_model)
    pad_heads = lambda v: jnp.pad(v, (0, LANES - n_heads)).reshape(1, LANES)

    w_in = ssm_w_in[0]
    n_main = d_inner + conv_dim
    w_dt = jnp.pad(w_in[:, n_main:], ((0, 0), (0, LANES - n_heads))).astype(BF16)
    proj, dt = _ssm_inproj(
        x2, ln_g[0].reshape(1, d_model), w_in.astype(BF16), w_dt,
        ssm_conv_w[0], ssm_conv_b[0].reshape(1, conv_dim), pad_heads(ssm_dt_bias[0]),
        seq_len=seq_len, d_inner=d_inner, conv_dim=conv_dim)
    d_skip = jnp.repeat(ssm_d[0], HEADDIM).reshape(1, d_inner)
    yn = _ssd_scan(proj, dt, pad_heads(ssm_a_log[0]), d_skip, ssm_norm_g[0].reshape(1, d_inner),
                   batch=batch, seq_len=seq_len, d_inner=d_inner)
    x1, hn1 = _outproj(yn, ssm_w_out[0].astype(BF16), x2, ln_g[1].reshape(1, d_model), final=False)

    pw_in = pool_w_in[0].astype(BF16)
    mixed = _pool_inproj(hn1, pw_in, col0=0, pooled=True, seq_len=seq_len, d_pool=d_pool)
    gate = _pool_inproj(hn1, pw_in, col0=d_pool, pooled=False, seq_len=seq_len, d_pool=d_pool)
    y1 = _pool_group(mixed, pool_w_group[0].astype(BF16), gate, pool_scale[0].reshape(1, d_pool))
    (out,) = _outproj(y1, pool_w_out[0].astype(BF16), x1, final_g.reshape(1, d_model), final=True)
    return out.reshape(batch, seq_len, d_model)
```

```python
import functools

import jax
import jax.numpy as jnp
from jax import lax
from jax.experimental import pallas as pl
from jax.experimental.pallas import tpu as pltpu

F32 = jnp.float32
BF16 = jnp.bfloat16

EPS = 1e-6
LOG2_E = 1.4426950408889634
LANES = 128
SUBLANES = 8
V7X_VMEM_BYTES = 64 * 1024 * 1024

SSD_CHUNK = 64
HEADDIM = 64
N_STATE = 128
N_GROUPS = 8
HEADS_PER_GROUP = 8
GROUP_WIDTH = HEADS_PER_GROUP * HEADDIM
HALF_WIDTH = GROUP_WIDTH // 2
CONV_TAPS = 4
CONV_HALO = SUBLANES
POOL_WINDOWS = (2, 4, 8, 16)
POOL_HALO = 16
CHUNK_PAIR = 2 * SSD_CHUNK
EPILOGUE_ROWS = 128


def _vmem_limit(block_bytes):
    return int(min(block_bytes + 20 * 1024 * 1024, V7X_VMEM_BYTES - 4 * 1024 * 1024))


def _rmsnorm(x, g):
    ms = jnp.mean(x * x, axis=-1, keepdims=True)
    return x * lax.rsqrt(ms + EPS) * g


def _silu_of_twice(h):
    return h + h * jnp.tanh(h)


def _silu(x):
    return _silu_of_twice(0.5 * x)


def _softplus(x):
    return jnp.maximum(x, 0.0) + jnp.log1p(jnp.exp(-jnp.abs(x)))


def _for_row_chunks(tm, rows, body):
    def it(c, carry):
        body(pl.multiple_of(c * rows, rows))
        return carry
    lax.fori_loop(0, tm // rows, it, 0)


def _matmul_below_halo(ext_ref, carry_ref, col, first_tile_of_seq, halo, tm, lhs, rhs):
    @pl.when(first_tile_of_seq)
    def _():
        carry_ref[col] = jnp.zeros(carry_ref.shape[1:], F32)

    ext_ref[0:halo, :] = carry_ref[col]
    ext_ref[halo:, :] = jnp.dot(lhs, rhs, preferred_element_type=F32)
    carry_ref[col] = ext_ref[tm:tm + halo, :]


def _ssm_inproj_kernel(x_ref, g_ref, w_ref, wdt_ref, cw_ref, cb_ref, dtb_ref,
                       out_ref, dt_ref, hn_ref, ext_ref, carry_ref,
                       *, n_z_tiles, tiles_per_seq):
    i = pl.program_id(0)
    j = pl.program_id(1)
    tm = out_ref.shape[0]

    @pl.when(j == 0)
    def _():
        hn = _rmsnorm(x_ref[...], g_ref[...]).astype(BF16)
        hn_ref[...] = hn
        d = jnp.dot(hn, wdt_ref[...], preferred_element_type=F32)
        dt_ref[...] = _softplus(d + dtb_ref[...])

    @pl.when(j < n_z_tiles)
    def _():
        out_ref[...] = jnp.dot(hn_ref[...], w_ref[...], preferred_element_type=F32)

    @pl.when(j >= n_z_tiles)
    def _():
        _matmul_below_halo(ext_ref, carry_ref, j - n_z_tiles, i % tiles_per_seq == 0, CONV_HALO, tm,
                           hn_ref[...], w_ref[...])
        cw = 0.5 * cw_ref[...]
        cb = 0.5 * cb_ref[...]

        def rows(r0):
            e = ext_ref[pl.ds(r0, EPILOGUE_ROWS + CONV_HALO), :]
            h = cb
            for k in range(CONV_TAPS):
                shift = CONV_TAPS - 1 - k
                h = h + (pltpu.roll(e, shift, 0) if shift else e) * cw[k:k + 1, :]
            out_ref[pl.ds(r0, EPILOGUE_ROWS), :] = _silu_of_twice(h[CONV_HALO:])

        _for_row_chunks(tm, EPILOGUE_ROWS, rows)


def _ssm_inproj(x2, g, w, w_dt, conv_w, conv_b, dt_bias, *, seq_len, d_inner, conv_dim, tm=1024, tn=512):
    m, d = x2.shape
    n_main = d_inner + conv_dim
    n_z_tiles = d_inner // tn
    dt_w = w_dt.shape[1]
    blocks = (2 * tm * d * 4 + tm * d * 2 + 2 * d * tn * 2 + 2 * d * dt_w * 2 + 2 * tm * tn * 4
              + 2 * tm * dt_w * 4 + (tm + CONV_HALO) * tn * 4)
    conv_idx = lambda i, j: (0, jnp.maximum(j - n_z_tiles, 0))
    return pl.pallas_call(
        functools.partial(_ssm_inproj_kernel, n_z_tiles=n_z_tiles, tiles_per_seq=seq_len // tm),
        grid=(m // tm, n_main // tn),
        in_specs=[
            pl.BlockSpec((tm, d), lambda i, j: (i, 0)),
            pl.BlockSpec((1, d), lambda i, j: (0, 0)),
            pl.BlockSpec((d, tn), lambda i, j: (0, j)),
            pl.BlockSpec((d, dt_w), lambda i, j: (0, 0)),
            pl.BlockSpec((CONV_TAPS, tn), conv_idx),
            pl.BlockSpec((1, tn), conv_idx),
            pl.BlockSpec((1, dt_w), lambda i, j: (0, 0)),
        ],
        out_specs=[
            pl.BlockSpec((tm, tn), lambda i, j: (i, j)),
            pl.BlockSpec((tm, dt_w), lambda i, j: (i, 0)),
        ],
        out_shape=[
            jax.ShapeDtypeStruct((m, n_main), F32),
            jax.ShapeDtypeStruct((m, dt_w), F32),
        ],
        scratch_shapes=[
            pltpu.VMEM((tm, d), BF16),
            pltpu.VMEM((tm + CONV_HALO, tn), F32),
            pltpu.VMEM((conv_dim // tn, CONV_HALO, tn), F32),
        ],
        compiler_params=pltpu.CompilerParams(
            dimension_semantics=("arbitrary", "arbitrary"),
            vmem_limit_bytes=_vmem_limit(blocks)),
        name="ssm_inproj",
    )(x2, g, w, w_dt, conv_w, conv_b, dt_bias)


def _pool_inproj_kernel(h_ref, w_ref, mixed_ref, gate_ref, hs_ref, ext_ref, carry_ref,
                        *, n_u_tiles, tiles_per_seq, tiles_per_group):
    i = pl.program_id(0)
    j = pl.program_id(1)
    tm, tn = mixed_ref.shape

    @pl.when(j == 0)
    def _():
        hs_ref[...] = h_ref[...]

    @pl.when(j >= n_u_tiles)
    def _():
        gate_ref[...] = jnp.dot(hs_ref[...], w_ref[...], preferred_element_type=F32)

    @pl.when(j < n_u_tiles)
    def _():
        _matmul_below_halo(ext_ref, carry_ref, j, i % tiles_per_seq == 0, POOL_HALO, tm,
                           hs_ref[...], w_ref[...])
        pos0 = (i % tiles_per_seq) * tm + 1

        for k, win in enumerate(POOL_WINDOWS):
            @pl.when(j // tiles_per_group == k)
            def _(win=win):
                def rows(r0):
                    e = ext_ref[pl.ds(r0, EPILOGUE_ROWS + POOL_HALO), :]
                    acc = e
                    span = 1
                    while span < win:
                        acc = acc + pltpu.roll(acc, span, 0)
                        span *= 2
                    pos = pos0 + r0 + lax.broadcasted_iota(jnp.int32, (EPILOGUE_ROWS, tn), 0)
                    cnt = jnp.minimum(pos, win).astype(F32)
                    mixed_ref[pl.ds(r0, EPILOGUE_ROWS), :] = (
                        acc[POOL_HALO:] / cnt - e[POOL_HALO:]).astype(mixed_ref.dtype)

                _for_row_chunks(tm, EPILOGUE_ROWS, rows)


def _pool_inproj(h, w, *, seq_len, d_pool, tm=1024, tn=512):
    m, d = h.shape
    n_u_tiles = d_pool // tn
    assert POOL_WINDOWS == tuple(2 << k for k in range(len(POOL_WINDOWS)))
    blocks = (2 * tm * d * 2 + tm * d * 2 + 2 * d * tn * 2 + 2 * tm * tn * 2 + 2 * tm * tn * 4
              + (tm + POOL_HALO) * tn * 4)
    return pl.pallas_call(
        functools.partial(_pool_inproj_kernel, n_u_tiles=n_u_tiles, tiles_per_seq=seq_len // tm,
                          tiles_per_group=d_pool // len(POOL_WINDOWS) // tn),
        grid=(m // tm, 2 * n_u_tiles),
        in_specs=[
            pl.BlockSpec((tm, d), lambda i, j: (i, 0)),
            pl.BlockSpec((d, tn), lambda i, j: (0, j)),
        ],
        out_specs=[
            pl.BlockSpec((tm, tn), lambda i, j: (i, jnp.minimum(j, n_u_tiles - 1))),
            pl.BlockSpec((tm, tn), lambda i, j: (i, jnp.maximum(j - n_u_tiles, 0))),
        ],
        out_shape=[
            jax.ShapeDtypeStruct((m, d_pool), BF16),
            jax.ShapeDtypeStruct((m, d_pool), F32),
        ],
        scratch_shapes=[
            pltpu.VMEM((tm, d), BF16),
            pltpu.VMEM((tm + POOL_HALO, tn), F32),
            pltpu.VMEM((n_u_tiles, POOL_HALO, tn), F32),
        ],
        compiler_params=pltpu.CompilerParams(
            dimension_semantics=("arbitrary", "arbitrary"),
            vmem_limit_bytes=_vmem_limit(blocks)),
        name="pool_inproj",
    )(h, w)


def _split3(x):
    hi = x.astype(BF16)
    r1 = x - hi.astype(F32)
    mid = r1.astype(BF16)
    lo = (r1 - mid.astype(F32)).astype(BF16)
    return hi, mid, lo


def _ssd_kernel(z_ref, xs_ref, bc_ref, dt_ref, alog_ref, dskip_ref, ng_ref,
                out_ref, state_ref, y_ref, ssq_ref, xdiag_ref):
    tb, d_inner = z_ref.shape
    q = SSD_CHUNK

    @pl.when(pl.program_id(1) == 0)
    def _():
        state_ref[...] = jnp.zeros_like(state_ref)

    @pl.when((pl.program_id(0) == 0) & (pl.program_id(1) == 0))
    def _():
        xdiag_ref[...] = jnp.zeros_like(xdiag_ref)

    a = -jnp.exp(alog_ref[...]) * LOG2_E

    r2 = lax.broadcasted_iota(jnp.int32, (CHUNK_PAIR, CHUNK_PAIR), 0)
    c2 = lax.broadcasted_iota(jnp.int32, (CHUNK_PAIR, CHUNK_PAIR), 1)
    tril_pair = jnp.where((r2 // q == c2 // q) & (r2 >= c2), 1.0, 0.0).astype(BF16)
    second_chunk = r2 >= q

    lrow = lax.broadcasted_iota(jnp.int32, (q, LANES), 0)
    lane = lax.broadcasted_iota(jnp.int32, (q, LANES), 1)
    lane_hi_i = (lane >= q).astype(jnp.int32)
    causal_pair = lrow >= (lane % q)
    lane_hi_row = lax.broadcasted_iota(jnp.int32, (1, LANES), 1) >= q

    def pair_body(p, carry):
        r0 = pl.multiple_of(p * CHUNK_PAIR, CHUNK_PAIR)
        dt2 = dt_ref[pl.ds(r0, CHUNK_PAIR), :]
        hi, mid, lo = _split3(dt2 * a)
        cs = (jnp.dot(tril_pair, hi, preferred_element_type=F32)
              + jnp.dot(tril_pair, mid, preferred_element_type=F32)
              + jnp.dot(tril_pair, lo, preferred_element_type=F32))
        last = jnp.where(second_chunk, cs[CHUNK_PAIR - 1:CHUNK_PAIR, :], cs[q - 1:q, :])
        wdt = jnp.exp2(last - cs) * dt2
        cs_t, dt_t, wd_t = cs.T, dt2.T, wdt.T
        cs_r, dt_r, wd_r = (pltpu.roll(v, q, 1) for v in (cs_t, dt_t, wd_t))

        for c in range(2):
            rows = pl.ds(r0 + c * q, q)
            cs_c = cs[c * q:(c + 1) * q]
            pick = (lambda t, r: (t, r)) if c == 0 else (lambda t, r: (r, t))
            (cs_lo, cs_hi), (dt_lo, dt_hi), (wd_lo, wd_hi) = pick(cs_t, cs_r), pick(dt_t, dt_r), pick(wd_t, wd_r)

            def pair_row(lo_src, hi_src, ha):
                return jnp.where(lane_hi_row, hi_src[ha + 1:ha + 2, :], lo_src[ha:ha + 1, :])

            ssq = jnp.zeros((q, LANES), F32)
            for g in range(N_GROUPS):
                h0 = g * HEADS_PER_GROUP
                cols_g = slice(g * GROUP_WIDTH, (g + 1) * GROUP_WIDTH)
                xs = xs_ref[rows, cols_g]
                bm32 = bc_ref[rows, g * N_STATE:(g + 1) * N_STATE]
                bm2 = jnp.concatenate([bm32, bm32], axis=0)
                cm = bc_ref[rows, (N_GROUPS + g) * N_STATE:(N_GROUPS + g + 1) * N_STATE].astype(BF16)
                cb2 = lax.dot_general(cm, bm2.astype(BF16), (((1,), (1,)), ((), ())),
                                      preferred_element_type=F32)
                bt2 = bm2.T
                top, bot, cols = [], [], []
                for p2 in range(HEADS_PER_GROUP // 2):
                    ha = h0 + 2 * p2
                    col = jnp.take_along_axis(cs_c, lane_hi_i + ha, axis=1, mode="promise_in_bounds")
                    decay = jnp.exp2(jnp.where(causal_pair, col - pair_row(cs_lo, cs_hi, ha), -jnp.inf))
                    top.append((cb2 * decay * pair_row(dt_lo, dt_hi, ha)).astype(BF16))
                    bot.append((bt2 * pair_row(wd_lo, wd_hi, ha)).astype(BF16))
                    cols.append(col)
                ecs = jnp.exp2(jnp.concatenate(cols, axis=1))
                xsb = xs.astype(BF16)
                ydiag, upd = [], []
                for k in range(2):
                    lhs = jnp.concatenate([jnp.concatenate(top[2 * k:2 * k + 2], axis=1),
                                           jnp.concatenate(bot[2 * k:2 * k + 2], axis=1)], axis=0)
                    slot = (c * N_GROUPS + g) * 2 + k
                    for r in range(HALF_WIDTH // HEADDIM):
                        blk = slice(r * HEADDIM, (r + 1) * HEADDIM)
                        xdiag_ref[slot, blk, blk] = xsb[:, k * HALF_WIDTH + r * HEADDIM:
                                                        k * HALF_WIDTH + (r + 1) * HEADDIM]
                    o = jnp.dot(lhs, xdiag_ref[slot], preferred_element_type=F32)
                    ydiag.append(o[:q])
                    upd.append(o[q:])
                s_g = state_ref[g]
                yoff = jnp.dot(cm, s_g.astype(BF16), preferred_element_type=F32)
                y = jnp.concatenate(ydiag, axis=1) + yoff * ecs
                y = y + dskip_ref[:, cols_g] * xs
                yg = y * _silu(z_ref[rows, cols_g])
                y_ref[rows, cols_g] = yg
                sq = yg * yg
                for t in range(GROUP_WIDTH // LANES):
                    ssq = ssq + sq[:, t * LANES:(t + 1) * LANES]
                state_ref[g] = s_g * ecs[q - 1:q, :] + jnp.concatenate(upd, axis=1)
            ssq_ref[rows, :] = ssq
        return carry

    lax.fori_loop(0, tb // CHUNK_PAIR, pair_body, 0)

    slab = 4 * SUBLANES

    def norm_body(s, carry):
        rr = pl.ds(pl.multiple_of(s * slab, slab), slab)
        ms = jnp.sum(ssq_ref[rr, :], axis=-1, keepdims=True) * (1.0 / d_inner)
        out_ref[rr, :] = (y_ref[rr, :] * lax.rsqrt(ms + EPS) * ng_ref[...]).astype(BF16)
        return carry

    lax.fori_loop(0, tb // slab, norm_body, 0)


def _ssd_scan(proj, dt, a_log, d_skip, norm_g, *, batch, seq_len, d_inner, tb=256):
    m = proj.shape[0]
    bc_w = 2 * N_GROUPS * N_STATE
    n_l = seq_len // tb
    row = lambda b, l: b * n_l + l
    blocks = (2 * 2 * tb * d_inner * 4 + 2 * tb * bc_w * 4 + 2 * tb * LANES * 4 + 2 * tb * d_inner * 2
              + N_GROUPS * N_STATE * GROUP_WIDTH * 4 + tb * d_inner * 4 + tb * LANES * 4
              + 4 * N_GROUPS * HALF_WIDTH * HALF_WIDTH * 2)
    return pl.pallas_call(
        _ssd_kernel,
        grid=(batch, n_l),
        in_specs=[
            pl.BlockSpec((tb, d_inner), lambda b, l: (row(b, l), 0)),
            pl.BlockSpec((tb, d_inner), lambda b, l: (row(b, l), 1)),
            pl.BlockSpec((tb, bc_w), lambda b, l: (row(b, l), 2 * d_inner // bc_w)),
            pl.BlockSpec((tb, LANES), lambda b, l: (row(b, l), 0)),
            pl.BlockSpec((1, LANES), lambda b, l: (0, 0)),
            pl.BlockSpec((1, d_inner), lambda b, l: (0, 0)),
            pl.BlockSpec((1, d_inner), lambda b, l: (0, 0)),
        ],
        out_specs=pl.BlockSpec((tb, d_inner), lambda b, l: (row(b, l), 0)),
        out_shape=jax.ShapeDtypeStruct((m, d_inner), BF16),
        scratch_shapes=[
            pltpu.VMEM((N_GROUPS, N_STATE, GROUP_WIDTH), F32),
            pltpu.VMEM((tb, d_inner), F32),
            pltpu.VMEM((tb, LANES), F32),
            pltpu.VMEM((2 * N_GROUPS * 2, HALF_WIDTH, HALF_WIDTH), BF16),
        ],
        compiler_params=pltpu.CompilerParams(
            dimension_semantics=("arbitrary", "arbitrary"),
            vmem_limit_bytes=_vmem_limit(blocks)),
        name="ssd_scan",
    )(proj, proj, proj, dt, a_log, d_skip, norm_g)


def _outproj_kernel(a_ref, w_ref, x_ref, g_ref, *out_refs, final):
    acc = jnp.dot(a_ref[...], w_ref[...], preferred_element_type=F32)
    x1 = x_ref[...] + acc
    hn = _rmsnorm(x1, g_ref[...])
    if final:
        out_refs[0][...] = hn
    else:
        out_refs[0][...] = x1
        out_refs[1][...] = hn.astype(BF16)


def _outproj(a, w, x2, g, *, final, tm=512):
    m, k = a.shape
    d = w.shape[1]
    row_spec = pl.BlockSpec((tm, d), lambda i: (i, 0))
    if final:
        out_specs = [row_spec]
        out_shape = [jax.ShapeDtypeStruct((m, d), F32)]
    else:
        out_specs = [row_spec, row_spec]
        out_shape = [jax.ShapeDtypeStruct((m, d), F32), jax.ShapeDtypeStruct((m, d), BF16)]
    blocks = 2 * tm * k * 2 + k * d * 2 + 2 * tm * d * 4 + 2 * tm * d * 4 + 2 * tm * d * 2
    return pl.pallas_call(
        functools.partial(_outproj_kernel, final=final),
        grid=(m // tm,),
        in_specs=[
            pl.BlockSpec((tm, k), lambda i: (i, 0)),
            pl.BlockSpec((k, d), lambda i: (0, 0), pipeline_mode=pl.Buffered(1)),
            row_spec,
            pl.BlockSpec((1, d), lambda i: (0, 0)),
        ],
        out_specs=out_specs,
        out_shape=out_shape,
        compiler_params=pltpu.CompilerParams(
            dimension_semantics=("arbitrary",),
            vmem_limit_bytes=_vmem_limit(blocks)),
        name="outproj_final" if final else "outproj",
    )(a, w, x2, g)


def _pool_group_kernel(m_ref, w_ref, gate_ref, scale_ref, out_ref):
    t = jnp.dot(m_ref[...], w_ref[...], preferred_element_type=F32)
    out_ref[...] = (t * scale_ref[...] * _silu(gate_ref[...])).astype(out_ref.dtype)


def _pool_group(mixed, w_group, gate, scale, *, tm=1024):
    m, d_pool = mixed.shape
    n_groups, gd, _ = w_group.shape
    blocks = 2 * tm * gd * 2 + 2 * gd * gd * 2 + 2 * tm * gd * 4 + 2 * tm * gd * 2
    return pl.pallas_call(
        _pool_group_kernel,
        grid=(n_groups, m // tm),
        in_specs=[
            pl.BlockSpec((tm, gd), lambda g, i: (i, g)),
            pl.BlockSpec((None, gd, gd), lambda g, i: (g, 0, 0)),
            pl.BlockSpec((tm, gd), lambda g, i: (i, g)),
            pl.BlockSpec((1, gd), lambda g, i: (0, g)),
        ],
        out_specs=pl.BlockSpec((tm, gd), lambda g, i: (i, g)),
        out_shape=jax.ShapeDtypeStruct((m, d_pool), BF16),
        compiler_params=pltpu.CompilerParams(
            dimension_semantics=("arbitrary", "arbitrary"),
            vmem_limit_bytes=_vmem_limit(blocks)),
        name="pool_group",
    )(mixed, w_group, gate, scale)


def kernel(x, ln_g, final_g, ssm_w_in, ssm_conv_w, ssm_conv_b, ssm_dt_bias, ssm_a_log, ssm_d, ssm_norm_g, ssm_w_out, pool_w_in, pool_w_group, pool_scale, pool_w_out):
    batch, seq_len, d_model = x.shape
    m = batch * seq_len
    d_inner = ssm_w_out.shape[1]
    conv_dim = ssm_conv_w.shape[2]
    n_heads = ssm_dt_bias.shape[1]
    d_pool = pool_w_out.shape[1]
    assert d_inner == N_GROUPS * GROUP_WIDTH and conv_dim == d_inner + 2 * N_GROUPS * N_STATE
    assert n_heads == N_GROUPS * HEADS_PER_GROUP and n_heads <= LANES
    assert ssm_conv_w.shape[1] == CONV_TAPS and pool_w_group.shape[1] == len(POOL_WINDOWS)

    x2 = x.reshape(m, d_model)
    pad_heads = lambda v: jnp.pad(v, (0, LANES - n_heads)).reshape(1, LANES)

    w_in = ssm_w_in[0]
    n_main = d_inner + conv_dim
    w_dt = jnp.pad(w_in[:, n_main:], ((0, 0), (0, LANES - n_heads))).astype(BF16)
    proj, dt = _ssm_inproj(
        x2, ln_g[0].reshape(1, d_model), w_in.astype(BF16), w_dt,
        ssm_conv_w[0], ssm_conv_b[0].reshape(1, conv_dim), pad_heads(ssm_dt_bias[0]),
        seq_len=seq_len, d_inner=d_inner, conv_dim=conv_dim)
    d_skip = jnp.repeat(ssm_d[0], HEADDIM).reshape(1, d_inner)
    yn = _ssd_scan(proj, dt, pad_heads(ssm_a_log[0]), d_skip, ssm_norm_g[0].reshape(1, d_inner),
                   batch=batch, seq_len=seq_len, d_inner=d_inner)
    x1, hn1 = _outproj(yn, ssm_w_out[0].astype(BF16), x2, ln_g[1].reshape(1, d_model), final=False)

    mixed, gate = _pool_inproj(hn1, pool_w_in[0].astype(BF16), seq_len=seq_len, d_pool=d_pool)
    y1 = _pool_group(mixed, pool_w_group[0].astype(BF16), gate, pool_scale[0].reshape(1, d_pool))
    (out,) = _outproj(y1, pool_w_out[0].astype(BF16), x1, final_g.reshape(1, d_model), final=True)
    return out.reshape(batch, seq_len, d_model)
```

```python
import functools

import jax
import jax.numpy as jnp
from jax import lax
from jax.experimental import pallas as pl
from jax.experimental.pallas import tpu as pltpu

F32 = jnp.float32
BF16 = jnp.bfloat16

EPS = 1e-6
LOG2_E = 1.4426950408889634
LANES = 128
SUBLANES = 8
V7X_VMEM_BYTES = 64 * 1024 * 1024

SSD_CHUNK = 64
HEADDIM = 64
N_STATE = 128
N_GROUPS = 8
HEADS_PER_GROUP = 8
GROUP_WIDTH = HEADS_PER_GROUP * HEADDIM
HALF_WIDTH = GROUP_WIDTH // 2
CONV_TAPS = 4
CONV_HALO = SUBLANES
POOL_WINDOWS = (2, 4, 8, 16)
POOL_HALO = 16
CHUNK_PAIR = 2 * SSD_CHUNK
EPILOGUE_ROWS = 128


def _vmem_limit(block_bytes):
    return int(min(block_bytes + 20 * 1024 * 1024, V7X_VMEM_BYTES - 4 * 1024 * 1024))


def _rmsnorm(x, g):
    ms = jnp.mean(x * x, axis=-1, keepdims=True)
    return x * lax.rsqrt(ms + EPS) * g


def _silu_of_twice(h):
    return h + h * jnp.tanh(h)


def _silu(x):
    return _silu_of_twice(0.5 * x)


def _softplus(x):
    return jnp.maximum(x, 0.0) + jnp.log1p(jnp.exp(-jnp.abs(x)))


def _for_row_chunks(tm, rows, body):
    def it(c, carry):
        body(pl.multiple_of(c * rows, rows))
        return carry
    lax.fori_loop(0, tm // rows, it, 0)


def _matmul_below_halo(ext_ref, carry_ref, col, first_tile_of_seq, halo, tm, lhs, rhs):
    @pl.when(first_tile_of_seq)
    def _():
        carry_ref[col] = jnp.zeros(carry_ref.shape[1:], F32)

    ext_ref[0:halo, :] = carry_ref[col]
    ext_ref[halo:, :] = jnp.dot(lhs, rhs, preferred_element_type=F32)
    carry_ref[col] = ext_ref[tm:tm + halo, :]


def _ssm_inproj_kernel(x_ref, g_ref, w_ref, wdt_ref, cw_ref, cb_ref, dtb_ref,
                       out_ref, dt_ref, hn_ref, ext_ref, carry_ref,
                       *, n_z_tiles, tiles_per_seq):
    i = pl.program_id(0)
    j = pl.program_id(1)
    tm = out_ref.shape[0]

    @pl.when(j == 0)
    def _():
        hn = _rmsnorm(x_ref[...], g_ref[...]).astype(BF16)
        hn_ref[...] = hn
        d = jnp.dot(hn, wdt_ref[...], preferred_element_type=F32)
        dt_ref[...] = _softplus(d + dtb_ref[...])

    @pl.when(j < n_z_tiles)
    def _():
        out_ref[...] = jnp.dot(hn_ref[...], w_ref[...], preferred_element_type=F32)

    @pl.when(j >= n_z_tiles)
    def _():
        _matmul_below_halo(ext_ref, carry_ref, j - n_z_tiles, i % tiles_per_seq == 0, CONV_HALO, tm,
                           hn_ref[...], w_ref[...])
        cw = 0.5 * cw_ref[...]
        cb = 0.5 * cb_ref[...]

        def rows(r0):
            e = ext_ref[pl.ds(r0, EPILOGUE_ROWS + CONV_HALO), :]
            h = cb
            for k in range(CONV_TAPS):
                shift = CONV_TAPS - 1 - k
                h = h + (pltpu.roll(e, shift, 0) if shift else e) * cw[k:k + 1, :]
            out_ref[pl.ds(r0, EPILOGUE_ROWS), :] = _silu_of_twice(h[CONV_HALO:])

        _for_row_chunks(tm, EPILOGUE_ROWS, rows)


def _ssm_inproj(x2, g, w, w_dt, conv_w, conv_b, dt_bias, *, seq_len, d_inner, conv_dim, tm=1024, tn=1024):
    m, d = x2.shape
    n_main = d_inner + conv_dim
    n_z_tiles = d_inner // tn
    dt_w = w_dt.shape[1]
    blocks = (2 * tm * d * 4 + tm * d * 2 + 2 * d * tn * 2 + 2 * d * dt_w * 2 + 2 * tm * tn * 4
              + 2 * tm * dt_w * 4 + (tm + CONV_HALO) * tn * 4)
    conv_idx = lambda i, j: (0, jnp.maximum(j - n_z_tiles, 0))
    return pl.pallas_call(
        functools.partial(_ssm_inproj_kernel, n_z_tiles=n_z_tiles, tiles_per_seq=seq_len // tm),
        grid=(m // tm, n_main // tn),
        in_specs=[
            pl.BlockSpec((tm, d), lambda i, j: (i, 0)),
            pl.BlockSpec((1, d), lambda i, j: (0, 0)),
            pl.BlockSpec((d, tn), lambda i, j: (0, j)),
            pl.BlockSpec((d, dt_w), lambda i, j: (0, 0)),
            pl.BlockSpec((CONV_TAPS, tn), conv_idx),
            pl.BlockSpec((1, tn), conv_idx),
            pl.BlockSpec((1, dt_w), lambda i, j: (0, 0)),
        ],
        out_specs=[
            pl.BlockSpec((tm, tn), lambda i, j: (i, j)),
            pl.BlockSpec((tm, dt_w), lambda i, j: (i, 0)),
        ],
        out_shape=[
            jax.ShapeDtypeStruct((m, n_main), F32),
            jax.ShapeDtypeStruct((m, dt_w), F32),
        ],
        scratch_shapes=[
            pltpu.VMEM((tm, d), BF16),
            pltpu.VMEM((tm + CONV_HALO, tn), F32),
            pltpu.VMEM((conv_dim // tn, CONV_HALO, tn), F32),
        ],
        compiler_params=pltpu.CompilerParams(
            dimension_semantics=("arbitrary", "arbitrary"),
            vmem_limit_bytes=_vmem_limit(blocks)),
        name="ssm_inproj",
    )(x2, g, w, w_dt, conv_w, conv_b, dt_bias)


def _pool_inproj_kernel(h_ref, w_ref, mixed_ref, gate_ref, hs_ref, ext_ref, carry_ref,
                        *, n_u_tiles, tiles_per_seq, tiles_per_group):
    i = pl.program_id(0)
    j = pl.program_id(1)
    tm, tn = mixed_ref.shape

    @pl.when(j == 0)
    def _():
        hs_ref[...] = h_ref[...]

    @pl.when(j >= n_u_tiles)
    def _():
        gate_ref[...] = jnp.dot(hs_ref[...], w_ref[...], preferred_element_type=F32)

    @pl.when(j < n_u_tiles)
    def _():
        _matmul_below_halo(ext_ref, carry_ref, j, i % tiles_per_seq == 0, POOL_HALO, tm,
                           hs_ref[...], w_ref[...])
        pos0 = (i % tiles_per_seq) * tm + 1

        for k, win in enumerate(POOL_WINDOWS):
            @pl.when(j // tiles_per_group == k)
            def _(win=win):
                def rows(r0):
                    e = ext_ref[pl.ds(r0, EPILOGUE_ROWS + POOL_HALO), :]
                    acc = e
                    span = 1
                    while span < win:
                        acc = acc + pltpu.roll(acc, span, 0)
                        span *= 2
                    pos = pos0 + r0 + lax.broadcasted_iota(jnp.int32, (EPILOGUE_ROWS, tn), 0)
                    cnt = jnp.minimum(pos, win).astype(F32)
                    mixed_ref[pl.ds(r0, EPILOGUE_ROWS), :] = (
                        acc[POOL_HALO:] / cnt - e[POOL_HALO:]).astype(mixed_ref.dtype)

                _for_row_chunks(tm, EPILOGUE_ROWS, rows)


def _pool_inproj(h, w, *, seq_len, d_pool, tm=1024, tn=1024):
    m, d = h.shape
    n_u_tiles = d_pool // tn
    assert POOL_WINDOWS == tuple(2 << k for k in range(len(POOL_WINDOWS)))
    blocks = (2 * tm * d * 2 + tm * d * 2 + 2 * d * tn * 2 + 2 * tm * tn * 2 + 2 * tm * tn * 4
              + (tm + POOL_HALO) * tn * 4)
    return pl.pallas_call(
        functools.partial(_pool_inproj_kernel, n_u_tiles=n_u_tiles, tiles_per_seq=seq_len // tm,
                          tiles_per_group=d_pool // len(POOL_WINDOWS) // tn),
        grid=(m // tm, 2 * n_u_tiles),
        in_specs=[
            pl.BlockSpec((tm, d), lambda i, j: (i, 0)),
            pl.BlockSpec((d, tn), lambda i, j: (0, j)),
        ],
        out_specs=[
            pl.BlockSpec((tm, tn), lambda i, j: (i, jnp.minimum(j, n_u_tiles - 1))),
            pl.BlockSpec((tm, tn), lambda i, j: (i, jnp.maximum(j - n_u_tiles, 0))),
        ],
        out_shape=[
            jax.ShapeDtypeStruct((m, d_pool), BF16),
            jax.ShapeDtypeStruct((m, d_pool), F32),
        ],
        scratch_shapes=[
            pltpu.VMEM((tm, d), BF16),
            pltpu.VMEM((tm + POOL_HALO, tn), F32),
            pltpu.VMEM((n_u_tiles, POOL_HALO, tn), F32),
        ],
        compiler_params=pltpu.CompilerParams(
            dimension_semantics=("arbitrary", "arbitrary"),
            vmem_limit_bytes=_vmem_limit(blocks)),
        name="pool_inproj",
    )(h, w)


def _split3(x):
    hi = x.astype(BF16)
    r1 = x - hi.astype(F32)
    mid = r1.astype(BF16)
    lo = (r1 - mid.astype(F32)).astype(BF16)
    return hi, mid, lo


def _ssd_kernel(z_ref, xs_ref, bc_ref, dt_ref, alog_ref, dskip_ref, ng_ref,
                out_ref, state_ref, y_ref, ssq_ref, xdiag_ref):
    tb, d_inner = z_ref.shape
    q = SSD_CHUNK

    @pl.when(pl.program_id(1) == 0)
    def _():
        state_ref[...] = jnp.zeros_like(state_ref)

    @pl.when((pl.program_id(0) == 0) & (pl.program_id(1) == 0))
    def _():
        xdiag_ref[...] = jnp.zeros_like(xdiag_ref)

    a = -jnp.exp(alog_ref[...]) * LOG2_E

    r2 = lax.broadcasted_iota(jnp.int32, (CHUNK_PAIR, CHUNK_PAIR), 0)
    c2 = lax.broadcasted_iota(jnp.int32, (CHUNK_PAIR, CHUNK_PAIR), 1)
    tril_pair = jnp.where((r2 // q == c2 // q) & (r2 >= c2), 1.0, 0.0).astype(BF16)
    second_chunk = r2 >= q

    lrow = lax.broadcasted_iota(jnp.int32, (q, LANES), 0)
    lane = lax.broadcasted_iota(jnp.int32, (q, LANES), 1)
    lane_hi_i = (lane >= q).astype(jnp.int32)
    causal_pair = lrow >= (lane % q)
    lane_hi_row = lax.broadcasted_iota(jnp.int32, (1, LANES), 1) >= q

    def pair_body(p, carry):
        r0 = pl.multiple_of(p * CHUNK_PAIR, CHUNK_PAIR)
        dt2 = dt_ref[pl.ds(r0, CHUNK_PAIR), :]
        hi, mid, lo = _split3(dt2 * a)
        cs = (jnp.dot(tril_pair, hi, preferred_element_type=F32)
              + jnp.dot(tril_pair, mid, preferred_element_type=F32)
              + jnp.dot(tril_pair, lo, preferred_element_type=F32))
        last = jnp.where(second_chunk, cs[CHUNK_PAIR - 1:CHUNK_PAIR, :], cs[q - 1:q, :])
        wdt = jnp.exp2(last - cs) * dt2
        cs_t, dt_t, wd_t = cs.T, dt2.T, wdt.T
        cs_r, dt_r, wd_r = (pltpu.roll(v, q, 1) for v in (cs_t, dt_t, wd_t))

        for c in range(2):
            rows = pl.ds(r0 + c * q, q)
            cs_c = cs[c * q:(c + 1) * q]
            pick = (lambda t, r: (t, r)) if c == 0 else (lambda t, r: (r, t))
            (cs_lo, cs_hi), (dt_lo, dt_hi), (wd_lo, wd_hi) = pick(cs_t, cs_r), pick(dt_t, dt_r), pick(wd_t, wd_r)

            def pair_row(lo_src, hi_src, ha):
                return jnp.where(lane_hi_row, hi_src[ha + 1:ha + 2, :], lo_src[ha:ha + 1, :])

            def early(g):
                cols_g = slice(g * GROUP_WIDTH, (g + 1) * GROUP_WIDTH)
                bm32 = bc_ref[rows, g * N_STATE:(g + 1) * N_STATE]
                bm2 = jnp.concatenate([bm32, bm32], axis=0)
                cm = bc_ref[rows, (N_GROUPS + g) * N_STATE:(N_GROUPS + g + 1) * N_STATE].astype(BF16)
                cb2 = lax.dot_general(cm, bm2.astype(BF16), (((1,), (1,)), ((), ())),
                                      preferred_element_type=F32)
                yoff = jnp.dot(cm, state_ref[g].astype(BF16), preferred_element_type=F32)
                xsb = xs_ref[rows, cols_g].astype(BF16)
                for k in range(2):
                    for r in range(HALF_WIDTH // HEADDIM):
                        blk = slice(r * HEADDIM, (r + 1) * HEADDIM)
                        xdiag_ref[(c * N_GROUPS + g) * 2 + k, blk, blk] = xsb[
                            :, k * HALF_WIDTH + r * HEADDIM:k * HALF_WIDTH + (r + 1) * HEADDIM]
                return cb2, bm2.T, yoff

            def late(g, cb2, bt2, yoff, ssq):
                h0 = g * HEADS_PER_GROUP
                cols_g = slice(g * GROUP_WIDTH, (g + 1) * GROUP_WIDTH)
                top, bot, cols = [], [], []
                for p2 in range(HEADS_PER_GROUP // 2):
                    ha = h0 + 2 * p2
                    col = jnp.take_along_axis(cs_c, lane_hi_i + ha, axis=1, mode="promise_in_bounds")
                    decay = jnp.exp2(jnp.where(causal_pair, col - pair_row(cs_lo, cs_hi, ha), -jnp.inf))
                    top.append((cb2 * decay * pair_row(dt_lo, dt_hi, ha)).astype(BF16))
                    bot.append((bt2 * pair_row(wd_lo, wd_hi, ha)).astype(BF16))
                    cols.append(col)
                ecs = jnp.exp2(jnp.concatenate(cols, axis=1))
                ydiag, upd = [], []
                for k in range(2):
                    lhs = jnp.concatenate([jnp.concatenate(top[2 * k:2 * k + 2], axis=1),
                                           jnp.concatenate(bot[2 * k:2 * k + 2], axis=1)], axis=0)
                    o = jnp.dot(lhs, xdiag_ref[(c * N_GROUPS + g) * 2 + k],
                                preferred_element_type=F32)
                    ydiag.append(o[:q])
                    upd.append(o[q:])
                y = jnp.concatenate(ydiag, axis=1) + yoff * ecs
                y = y + dskip_ref[:, cols_g] * xs_ref[rows, cols_g]
                yg = y * _silu(z_ref[rows, cols_g])
                y_ref[rows, cols_g] = yg
                sq = yg * yg
                for t in range(GROUP_WIDTH // LANES):
                    ssq = ssq + sq[:, t * LANES:(t + 1) * LANES]
                state_ref[g] = state_ref[g] * ecs[q - 1:q, :] + jnp.concatenate(upd, axis=1)
                return ssq

            ssq = jnp.zeros((q, LANES), F32)
            staged = early(0)
            for g in range(N_GROUPS):
                current = staged
                if g + 1 < N_GROUPS:
                    staged = early(g + 1)
                ssq = late(g, *current, ssq)
            ssq_ref[rows, :] = ssq
        return carry

    lax.fori_loop(0, tb // CHUNK_PAIR, pair_body, 0)

    slab = 4 * SUBLANES

    def norm_body(s, carry):
        rr = pl.ds(pl.multiple_of(s * slab, slab), slab)
        ms = jnp.sum(ssq_ref[rr, :], axis=-1, keepdims=True) * (1.0 / d_inner)
        out_ref[rr, :] = (y_ref[rr, :] * lax.rsqrt(ms + EPS) * ng_ref[...]).astype(BF16)
        return carry

    lax.fori_loop(0, tb // slab, norm_body, 0)


def _ssd_scan(proj, dt, a_log, d_skip, norm_g, *, batch, seq_len, d_inner, tb=256):
    m = proj.shape[0]
    bc_w = 2 * N_GROUPS * N_STATE
    n_l = seq_len // tb
    row = lambda b, l: b * n_l + l
    blocks = (2 * 2 * tb * d_inner * 4 + 2 * tb * bc_w * 4 + 2 * tb * LANES * 4 + 2 * tb * d_inner * 2
              + N_GROUPS * N_STATE * GROUP_WIDTH * 4 + tb * d_inner * 4 + tb * LANES * 4
              + 4 * N_GROUPS * HALF_WIDTH * HALF_WIDTH * 2)
    return pl.pallas_call(
        _ssd_kernel,
        grid=(batch, n_l),
        in_specs=[
            pl.BlockSpec((tb, d_inner), lambda b, l: (row(b, l), 0)),
            pl.BlockSpec((tb, d_inner), lambda b, l: (row(b, l), 1)),
            pl.BlockSpec((tb, bc_w), lambda b, l: (row(b, l), 2 * d_inner // bc_w)),
            pl.BlockSpec((tb, LANES), lambda b, l: (row(b, l), 0)),
            pl.BlockSpec((1, LANES), lambda b, l: (0, 0)),
            pl.BlockSpec((1, d_inner), lambda b, l: (0, 0)),
            pl.BlockSpec((1, d_inner), lambda b, l: (0, 0)),
        ],
        out_specs=pl.BlockSpec((tb, d_inner), lambda b, l: (row(b, l), 0)),
        out_shape=jax.ShapeDtypeStruct((m, d_inner), BF16),
        scratch_shapes=[
            pltpu.VMEM((N_GROUPS, N_STATE, GROUP_WIDTH), F32),
            pltpu.VMEM((tb, d_inner), F32),
            pltpu.VMEM((tb, LANES), F32),
            pltpu.VMEM((2 * N_GROUPS * 2, HALF_WIDTH, HALF_WIDTH), BF16),
        ],
        compiler_params=pltpu.CompilerParams(
            dimension_semantics=("arbitrary", "arbitrary"),
            vmem_limit_bytes=_vmem_limit(blocks)),
        name="ssd_scan",
    )(proj, proj, proj, dt, a_log, d_skip, norm_g)


def _outproj_kernel(a_ref, w_ref, x_ref, g_ref, *out_refs, final):
    acc = jnp.dot(a_ref[...], w_ref[...], preferred_element_type=F32)
    x1 = x_ref[...] + acc
    hn = _rmsnorm(x1, g_ref[...])
    if final:
        out_refs[0][...] = hn
    else:
        out_refs[0][...] = x1
        out_refs[1][...] = hn.astype(BF16)


def _outproj(a, w, x2, g, *, final, tm=512):
    m, k = a.shape
    d = w.shape[1]
    row_spec = pl.BlockSpec((tm, d), lambda i: (i, 0))
    if final:
        out_specs = [row_spec]
        out_shape = [jax.ShapeDtypeStruct((m, d), F32)]
    else:
        out_specs = [row_spec, row_spec]
        out_shape = [jax.ShapeDtypeStruct((m, d), F32), jax.ShapeDtypeStruct((m, d), BF16)]
    blocks = 2 * tm * k * 2 + k * d * 2 + 2 * tm * d * 4 + 2 * tm * d * 4 + 2 * tm * d * 2
    return pl.pallas_call(
        functools.partial(_outproj_kernel, final=final),
        grid=(m // tm,),
        in_specs=[
            pl.BlockSpec((tm, k), lambda i: (i, 0)),
            pl.BlockSpec((k, d), lambda i: (0, 0), pipeline_mode=pl.Buffered(1)),
            row_spec,
            pl.BlockSpec((1, d), lambda i: (0, 0)),
        ],
        out_specs=out_specs,
        out_shape=out_shape,
        compiler_params=pltpu.CompilerParams(
            dimension_semantics=("arbitrary",),
            vmem_limit_bytes=_vmem_limit(blocks)),
        name="outproj_final" if final else "outproj",
    )(a, w, x2, g)


def _pool_group_kernel(m_ref, w_ref, gate_ref, scale_ref, out_ref, wb_ref):
    @pl.when(pl.program_id(1) == 0)
    def _():
        wb_ref[...] = w_ref[...].astype(BF16)

    t = jnp.dot(m_ref[...], wb_ref[...], preferred_element_type=F32)
    out_ref[...] = (t * scale_ref[...] * _silu(gate_ref[...])).astype(out_ref.dtype)


def _pool_group(mixed, w_group, gate, scale, *, tm=1024):
    m, d_pool = mixed.shape
    n_groups, gd, _ = w_group.shape
    blocks = 2 * tm * gd * 2 + 2 * gd * gd * 4 + gd * gd * 2 + 2 * tm * gd * 4 + 2 * tm * gd * 2
    return pl.pallas_call(
        _pool_group_kernel,
        grid=(n_groups, m // tm),
        in_specs=[
            pl.BlockSpec((tm, gd), lambda g, i: (i, g)),
            pl.BlockSpec((None, gd, gd), lambda g, i: (g, 0, 0)),
            pl.BlockSpec((tm, gd), lambda g, i: (i, g)),
            pl.BlockSpec((1, gd), lambda g, i: (0, g)),
        ],
        out_specs=pl.BlockSpec((tm, gd), lambda g, i: (i, g)),
        out_shape=jax.ShapeDtypeStruct((m, d_pool), BF16),
        scratch_shapes=[pltpu.VMEM((gd, gd), BF16)],
        compiler_params=pltpu.CompilerParams(
            dimension_semantics=("arbitrary", "arbitrary"),
            vmem_limit_bytes=_vmem_limit(blocks)),
        name="pool_group",
    )(mixed, w_group, gate, scale)


def kernel(x, ln_g, final_g, ssm_w_in, ssm_conv_w, ssm_conv_b, ssm_dt_bias, ssm_a_log, ssm_d, ssm_norm_g, ssm_w_out, pool_w_in, pool_w_group, pool_scale, pool_w_out):
    batch, seq_len, d_model = x.shape
    m = batch * seq_len
    d_inner = ssm_w_out.shape[1]
    conv_dim = ssm_conv_w.shape[2]
    n_heads = ssm_dt_bias.shape[1]
    d_pool = pool_w_out.shape[1]
    assert d_inner == N_GROUPS * GROUP_WIDTH and conv_dim == d_inner + 2 * N_GROUPS * N_STATE
    assert n_heads == N_GROUPS * HEADS_PER_GROUP and n_heads <= LANES
    assert ssm_conv_w.shape[1] == CONV_TAPS and pool_w_group.shape[1] == len(POOL_WINDOWS)

    x2 = x.reshape(m, d_model)
    pad_heads = lambda v: jnp.pad(v, (0, LANES - n_heads)).reshape(1, LANES)

    w_in = ssm_w_in[0]
    n_main = d_inner + conv_dim
    w_dt = jnp.pad(w_in[:, n_main:], ((0, 0), (0, LANES - n_heads))).astype(BF16)
    proj, dt = _ssm_inproj(
        x2, ln_g[0].reshape(1, d_model), w_in.astype(BF16), w_dt,
        ssm_conv_w[0], ssm_conv_b[0].reshape(1, conv_dim), pad_heads(ssm_dt_bias[0]),
        seq_len=seq_len, d_inner=d_inner, conv_dim=conv_dim)
    d_skip = jnp.repeat(ssm_d[0], HEADDIM).reshape(1, d_inner)
    yn = _ssd_scan(proj, dt, pad_heads(ssm_a_log[0]), d_skip, ssm_norm_g[0].reshape(1, d_inner),
                   batch=batch, seq_len=seq_len, d_inner=d_inner)
    x1, hn1 = _outproj(yn, ssm_w_out[0].astype(BF16), x2, ln_g[1].reshape(1, d_model), final=False)

    mixed, gate = _pool_inproj(hn1, pool_w_in[0].astype(BF16), seq_len=seq_len, d_pool=d_pool)
    y1 = _pool_group(mixed, pool_w_group[0], gate, pool_scale[0].reshape(1, d_pool))
    (out,) = _outproj(y1, pool_w_out[0].astype(BF16), x1, final_g.reshape(1, d_model), final=True)
    return out.reshape(batch, seq_len, d_model)
```

```python
import functools

import jax
import jax.numpy as jnp
from jax import lax
from jax.experimental import pallas as pl
from jax.experimental.pallas import tpu as pltpu

F32 = jnp.float32
BF16 = jnp.bfloat16

EPS = 1e-6
LOG2_E = 1.4426950408889634
LANES = 128
SUBLANES = 8
V7X_VMEM_BYTES = 64 * 1024 * 1024

SSD_CHUNK = 64
HEADDIM = 64
N_STATE = 128
N_GROUPS = 8
HEADS_PER_GROUP = 8
GROUP_WIDTH = HEADS_PER_GROUP * HEADDIM
HALF_WIDTH = GROUP_WIDTH // 2
CONV_TAPS = 4
CONV_HALO = SUBLANES
POOL_WINDOWS = (2, 4, 8, 16)
POOL_HALO = 16
CHUNK_PAIR = 2 * SSD_CHUNK
EPILOGUE_ROWS = 128


def _vmem_limit(block_bytes):
    return int(min(block_bytes + 20 * 1024 * 1024, V7X_VMEM_BYTES - 4 * 1024 * 1024))


def _rmsnorm(x, g):
    ms = jnp.mean(x * x, axis=-1, keepdims=True)
    return x * lax.rsqrt(ms + EPS) * g


def _silu_of_twice(h):
    return h + h * jnp.tanh(h)


def _silu(x):
    return _silu_of_twice(0.5 * x)


def _softplus(x):
    return jnp.maximum(x, 0.0) + jnp.log1p(jnp.exp(-jnp.abs(x)))


def _for_row_chunks(tm, rows, body):
    def it(c, carry):
        body(pl.multiple_of(c * rows, rows))
        return carry
    lax.fori_loop(0, tm // rows, it, 0)


def _row_groups(v):
    return v.reshape(v.shape[0] // SUBLANES, SUBLANES, v.shape[1])


def _shift_rows_down(g3, s, keep_first=False):
    sub = lax.broadcasted_iota(jnp.int32, (1, SUBLANES, g3.shape[2]), 1)
    moved = pltpu.roll(jnp.where(sub >= SUBLANES - s, g3[:-1], g3[1:]), s, 1)
    return jnp.concatenate([pltpu.roll(g3[:1], s, 1), moved], axis=0) if keep_first else moved


def _matmul_below_halo(ext_ref, carry_ref, col, first_tile_of_seq, halo, tm, lhs, rhs):
    @pl.when(first_tile_of_seq)
    def _():
        carry_ref[col] = jnp.zeros(carry_ref.shape[1:], F32)

    ext_ref[0:halo, :] = carry_ref[col]
    ext_ref[halo:, :] = jnp.dot(lhs, rhs, preferred_element_type=F32)
    carry_ref[col] = ext_ref[tm:tm + halo, :]


def _ssm_inproj_kernel(x_ref, g_ref, w_ref, wdt_ref, cw_ref, cb_ref, dtb_ref,
                       out_ref, dt_ref, hn_ref, ext_ref, carry_ref, taps_ref,
                       *, n_z_tiles, tiles_per_seq):
    i = pl.program_id(0)
    j = pl.program_id(1)
    tm = out_ref.shape[0]

    @pl.when(j == 0)
    def _():
        hn = _rmsnorm(x_ref[...], g_ref[...]).astype(BF16)
        hn_ref[...] = hn
        d = jnp.dot(hn, wdt_ref[...], preferred_element_type=F32)
        dt_ref[...] = _softplus(d + dtb_ref[...])

    @pl.when(j < n_z_tiles)
    def _():
        out_ref[...] = jnp.dot(hn_ref[...], w_ref[...], preferred_element_type=F32)

    @pl.when(j >= n_z_tiles)
    def _():
        _matmul_below_halo(ext_ref, carry_ref, j - n_z_tiles, i % tiles_per_seq == 0, CONV_HALO, tm,
                           hn_ref[...], w_ref[...])
        tn = out_ref.shape[1]
        for k in range(CONV_TAPS):
            taps_ref[k] = jnp.broadcast_to(0.5 * cw_ref[k:k + 1, :], (SUBLANES, tn))
        taps_ref[CONV_TAPS] = jnp.broadcast_to(0.5 * cb_ref[...], (SUBLANES, tn))

        def rows(r0):
            e = _row_groups(ext_ref[pl.ds(r0, EPILOGUE_ROWS + CONV_HALO), :])
            h = taps_ref[CONV_TAPS]
            for k in range(CONV_TAPS):
                shift = CONV_TAPS - 1 - k
                h = h + (_shift_rows_down(e, shift) if shift else e[1:]) * taps_ref[k]
            out_ref[pl.ds(r0, EPILOGUE_ROWS), :] = _silu_of_twice(h).reshape(EPILOGUE_ROWS, tn)

        _for_row_chunks(tm, EPILOGUE_ROWS, rows)


def _ssm_inproj(x2, g, w, w_dt, conv_w, conv_b, dt_bias, *, seq_len, d_inner, conv_dim, tm=1024, tn=1024):
    m, d = x2.shape
    n_main = d_inner + conv_dim
    n_z_tiles = d_inner // tn
    dt_w = w_dt.shape[1]
    blocks = (2 * tm * d * 4 + tm * d * 2 + 2 * d * tn * 2 + 2 * d * dt_w * 2 + 2 * tm * tn * 4
              + 2 * tm * dt_w * 4 + (tm + CONV_HALO) * tn * 4)
    conv_idx = lambda i, j: (0, jnp.maximum(j - n_z_tiles, 0))
    return pl.pallas_call(
        functools.partial(_ssm_inproj_kernel, n_z_tiles=n_z_tiles, tiles_per_seq=seq_len // tm),
        grid=(m // tm, n_main // tn),
        in_specs=[
            pl.BlockSpec((tm, d), lambda i, j: (i, 0)),
            pl.BlockSpec((1, d), lambda i, j: (0, 0)),
            pl.BlockSpec((d, tn), lambda i, j: (0, j)),
            pl.BlockSpec((d, dt_w), lambda i, j: (0, 0)),
            pl.BlockSpec((CONV_TAPS, tn), conv_idx),
            pl.BlockSpec((1, tn), conv_idx),
            pl.BlockSpec((1, dt_w), lambda i, j: (0, 0)),
        ],
        out_specs=[
            pl.BlockSpec((tm, tn), lambda i, j: (i, j)),
            pl.BlockSpec((tm, dt_w), lambda i, j: (i, 0)),
        ],
        out_shape=[
            jax.ShapeDtypeStruct((m, n_main), F32),
            jax.ShapeDtypeStruct((m, dt_w), F32),
        ],
        scratch_shapes=[
            pltpu.VMEM((tm, d), BF16),
            pltpu.VMEM((tm + CONV_HALO, tn), F32),
            pltpu.VMEM((conv_dim // tn, CONV_HALO, tn), F32),
            pltpu.VMEM((CONV_TAPS + 1, SUBLANES, tn), F32),
        ],
        compiler_params=pltpu.CompilerParams(
            dimension_semantics=("arbitrary", "arbitrary"),
            vmem_limit_bytes=_vmem_limit(blocks)),
        name="ssm_inproj",
    )(x2, g, w, w_dt, conv_w, conv_b, dt_bias)


def _pool_inproj_kernel(h_ref, w_ref, mixed_ref, gate_ref, hs_ref, ext_ref, carry_ref,
                        *, n_u_tiles, tiles_per_seq, tiles_per_group):
    i = pl.program_id(0)
    j = pl.program_id(1)
    tm, tn = mixed_ref.shape

    @pl.when(j == 0)
    def _():
        hs_ref[...] = h_ref[...]

    @pl.when(j >= n_u_tiles)
    def _():
        gate_ref[...] = jnp.dot(hs_ref[...], w_ref[...], preferred_element_type=F32)

    @pl.when(j < n_u_tiles)
    def _():
        _matmul_below_halo(ext_ref, carry_ref, j, i % tiles_per_seq == 0, POOL_HALO, tm,
                           hs_ref[...], w_ref[...])
        pos0 = (i % tiles_per_seq) * tm + 1

        for k, win in enumerate(POOL_WINDOWS):
            @pl.when(j // tiles_per_group == k)
            def _(win=win):
                def rows(r0):
                    e = _row_groups(ext_ref[pl.ds(r0, EPILOGUE_ROWS + POOL_HALO), :])
                    acc = e[1:] + e[:-1] if win > SUBLANES else e[1:]
                    span = 1
                    while span < min(win, SUBLANES):
                        acc = acc + _shift_rows_down(acc, span, keep_first=True)
                        span *= 2
                    shape = (EPILOGUE_ROWS // SUBLANES, SUBLANES, tn)
                    pos = (pos0 + r0 + SUBLANES * lax.broadcasted_iota(jnp.int32, shape, 0)
                           + lax.broadcasted_iota(jnp.int32, shape, 1))
                    cnt = jnp.minimum(pos, win).astype(F32)
                    mixed_ref[pl.ds(r0, EPILOGUE_ROWS), :] = (
                        acc[1:] / cnt - e[2:]).reshape(EPILOGUE_ROWS, tn).astype(mixed_ref.dtype)

                _for_row_chunks(tm, EPILOGUE_ROWS, rows)


def _pool_inproj(h, w, *, seq_len, d_pool, tm=1024, tn=1024):
    m, d = h.shape
    n_u_tiles = d_pool // tn
    assert POOL_WINDOWS == tuple(2 << k for k in range(len(POOL_WINDOWS)))
    blocks = (2 * tm * d * 2 + tm * d * 2 + 2 * d * tn * 2 + 2 * tm * tn * 2 + 2 * tm * tn * 4
              + (tm + POOL_HALO) * tn * 4)
    return pl.pallas_call(
        functools.partial(_pool_inproj_kernel, n_u_tiles=n_u_tiles, tiles_per_seq=seq_len // tm,
                          tiles_per_group=d_pool // len(POOL_WINDOWS) // tn),
        grid=(m // tm, 2 * n_u_tiles),
        in_specs=[
            pl.BlockSpec((tm, d), lambda i, j: (i, 0)),
            pl.BlockSpec((d, tn), lambda i, j: (0, j)),
        ],
        out_specs=[
            pl.BlockSpec((tm, tn), lambda i, j: (i, jnp.minimum(j, n_u_tiles - 1))),
            pl.BlockSpec((tm, tn), lambda i, j: (i, jnp.maximum(j - n_u_tiles, 0))),
        ],
        out_shape=[
            jax.ShapeDtypeStruct((m, d_pool), BF16),
            jax.ShapeDtypeStruct((m, d_pool), F32),
        ],
        scratch_shapes=[
            pltpu.VMEM((tm, d), BF16),
            pltpu.VMEM((tm + POOL_HALO, tn), F32),
            pltpu.VMEM((n_u_tiles, POOL_HALO, tn), F32),
        ],
        compiler_params=pltpu.CompilerParams(
            dimension_semantics=("arbitrary", "arbitrary"),
            vmem_limit_bytes=_vmem_limit(blocks)),
        name="pool_inproj",
    )(h, w)


def _split3(x):
    hi = x.astype(BF16)
    r1 = x - hi.astype(F32)
    mid = r1.astype(BF16)
    lo = (r1 - mid.astype(F32)).astype(BF16)
    return hi, mid, lo


def _ssd_kernel(z_ref, xs_ref, bc_ref, dt_ref, alog_ref, dskip_ref, ng_ref,
                out_ref, state_ref, y_ref, ssq_ref, xdiag_ref):
    tb, d_inner = z_ref.shape
    q = SSD_CHUNK

    @pl.when(pl.program_id(1) == 0)
    def _():
        state_ref[...] = jnp.zeros_like(state_ref)

    @pl.when((pl.program_id(0) == 0) & (pl.program_id(1) == 0))
    def _():
        xdiag_ref[...] = jnp.zeros_like(xdiag_ref)

    a = -jnp.exp(alog_ref[...]) * LOG2_E

    r2 = lax.broadcasted_iota(jnp.int32, (CHUNK_PAIR, CHUNK_PAIR), 0)
    c2 = lax.broadcasted_iota(jnp.int32, (CHUNK_PAIR, CHUNK_PAIR), 1)
    tril_pair = jnp.where((r2 // q == c2 // q) & (r2 >= c2), 1.0, 0.0).astype(BF16)
    second_chunk = r2 >= q

    lrow = lax.broadcasted_iota(jnp.int32, (q, LANES), 0)
    lane = lax.broadcasted_iota(jnp.int32, (q, LANES), 1)
    lane_hi_i = (lane >= q).astype(jnp.int32)
    causal_pair = lrow >= (lane % q)
    lane_hi_row = lax.broadcasted_iota(jnp.int32, (1, LANES), 1) >= q

    def pair_body(p, carry):
        r0 = pl.multiple_of(p * CHUNK_PAIR, CHUNK_PAIR)
        dt2 = dt_ref[pl.ds(r0, CHUNK_PAIR), :]
        hi, mid, lo = _split3(dt2 * a)
        cs = (jnp.dot(tril_pair, hi, preferred_element_type=F32)
              + jnp.dot(tril_pair, mid, preferred_element_type=F32)
              + jnp.dot(tril_pair, lo, preferred_element_type=F32))
        last = jnp.where(second_chunk, cs[CHUNK_PAIR - 1:CHUNK_PAIR, :], cs[q - 1:q, :])
        wdt = jnp.exp2(last - cs) * dt2
        cs_t, dt_t, wd_t = cs.T, dt2.T, wdt.T
        cs_r, dt_r, wd_r = (pltpu.roll(v, q, 1) for v in (cs_t, dt_t, wd_t))

        for c in range(2):
            rows = pl.ds(r0 + c * q, q)
            cs_c = cs[c * q:(c + 1) * q]
            pick = (lambda t, r: (t, r)) if c == 0 else (lambda t, r: (r, t))
            (cs_lo, cs_hi), (dt_lo, dt_hi), (wd_lo, wd_hi) = pick(cs_t, cs_r), pick(dt_t, dt_r), pick(wd_t, wd_r)

            def pair_row(lo_src, hi_src, ha):
                return jnp.where(lane_hi_row, hi_src[ha + 1:ha + 2, :], lo_src[ha:ha + 1, :])

            def early(g):
                cols_g = slice(g * GROUP_WIDTH, (g + 1) * GROUP_WIDTH)
                bm32 = bc_ref[rows, g * N_STATE:(g + 1) * N_STATE]
                bm2 = jnp.concatenate([bm32, bm32], axis=0)
                cm = bc_ref[rows, (N_GROUPS + g) * N_STATE:(N_GROUPS + g + 1) * N_STATE].astype(BF16)
                cb2 = lax.dot_general(cm, bm2.astype(BF16), (((1,), (1,)), ((), ())),
                                      preferred_element_type=F32)
                yoff = jnp.dot(cm, state_ref[g].astype(BF16), preferred_element_type=F32)
                xsb = xs_ref[rows, cols_g].astype(BF16)
                for k in range(2):
                    for r in range(HALF_WIDTH // HEADDIM):
                        blk = slice(r * HEADDIM, (r + 1) * HEADDIM)
                        xdiag_ref[(c * N_GROUPS + g) * 2 + k, blk, blk] = xsb[
                            :, k * HALF_WIDTH + r * HEADDIM:k * HALF_WIDTH + (r + 1) * HEADDIM]
                return cb2, bm2.T, yoff

            def late(g, cb2, bt2, yoff, ssq):
                h0 = g * HEADS_PER_GROUP
                cols_g = slice(g * GROUP_WIDTH, (g + 1) * GROUP_WIDTH)
                top, bot, cols = [], [], []
                for p2 in range(HEADS_PER_GROUP // 2):
                    ha = h0 + 2 * p2
                    col = jnp.take_along_axis(cs_c, lane_hi_i + ha, axis=1, mode="promise_in_bounds")
                    decay = jnp.exp2(jnp.where(causal_pair, col - pair_row(cs_lo, cs_hi, ha), -jnp.inf))
                    top.append((cb2 * decay * pair_row(dt_lo, dt_hi, ha)).astype(BF16))
                    bot.append((bt2 * pair_row(wd_lo, wd_hi, ha)).astype(BF16))
                    cols.append(col)
                ecs = jnp.exp2(jnp.concatenate(cols, axis=1))
                ydiag, upd = [], []
                for k in range(2):
                    lhs = jnp.concatenate([jnp.concatenate(top[2 * k:2 * k + 2], axis=1),
                                           jnp.concatenate(bot[2 * k:2 * k + 2], axis=1)], axis=0)
                    o = jnp.dot(lhs, xdiag_ref[(c * N_GROUPS + g) * 2 + k],
                                preferred_element_type=F32)
                    ydiag.append(o[:q])
                    upd.append(o[q:])
                y = jnp.concatenate(ydiag, axis=1) + yoff * ecs
                y = y + dskip_ref[:, cols_g] * xs_ref[rows, cols_g]
                yg = y * _silu(z_ref[rows, cols_g])
                y_ref[rows, cols_g] = yg
                sq = yg * yg
                for t in range(GROUP_WIDTH // LANES):
                    ssq = ssq + sq[:, t * LANES:(t + 1) * LANES]
                state_ref[g] = state_ref[g] * ecs[q - 1:q, :] + jnp.concatenate(upd, axis=1)
                return ssq

            ssq = jnp.zeros((q, LANES), F32)
            staged = early(0)
            for g in range(N_GROUPS):
                current = staged
                if g + 1 < N_GROUPS:
                    staged = early(g + 1)
                ssq = late(g, *current, ssq)
            ssq_ref[rows, :] = ssq
        return carry

    lax.fori_loop(0, tb // CHUNK_PAIR, pair_body, 0)

    ms = jnp.sum(ssq_ref[...], axis=-1, keepdims=True) * (1.0 / d_inner)
    ssq_ref[...] = jnp.broadcast_to(lax.rsqrt(ms + EPS), ssq_ref.shape)

    slab = 4 * SUBLANES

    def norm_body(s, carry):
        rr = pl.ds(pl.multiple_of(s * slab, slab), slab)
        inv_rms = jnp.tile(ssq_ref[rr, :], (1, d_inner // LANES))
        out_ref[rr, :] = (y_ref[rr, :] * inv_rms * ng_ref[...]).astype(BF16)
        return carry

    lax.fori_loop(0, tb // slab, norm_body, 0)


def _ssd_scan(proj, dt, a_log, d_skip, norm_g, *, batch, seq_len, d_inner, tb=256):
    m = proj.shape[0]
    bc_w = 2 * N_GROUPS * N_STATE
    n_l = seq_len // tb
    row = lambda b, l: b * n_l + l
    blocks = (2 * 2 * tb * d_inner * 4 + 2 * tb * bc_w * 4 + 2 * tb * LANES * 4 + 2 * tb * d_inner * 2
              + N_GROUPS * N_STATE * GROUP_WIDTH * 4 + tb * d_inner * 4 + tb * LANES * 4
              + 4 * N_GROUPS * HALF_WIDTH * HALF_WIDTH * 2)
    return pl.pallas_call(
        _ssd_kernel,
        grid=(batch, n_l),
        in_specs=[
            pl.BlockSpec((tb, d_inner), lambda b, l: (row(b, l), 0)),
            pl.BlockSpec((tb, d_inner), lambda b, l: (row(b, l), 1)),
            pl.BlockSpec((tb, bc_w), lambda b, l: (row(b, l), 2 * d_inner // bc_w)),
            pl.BlockSpec((tb, LANES), lambda b, l: (row(b, l), 0)),
            pl.BlockSpec((1, LANES), lambda b, l: (0, 0)),
            pl.BlockSpec((1, d_inner), lambda b, l: (0, 0)),
            pl.BlockSpec((1, d_inner), lambda b, l: (0, 0)),
        ],
        out_specs=pl.BlockSpec((tb, d_inner), lambda b, l: (row(b, l), 0)),
        out_shape=jax.ShapeDtypeStruct((m, d_inner), BF16),
        scratch_shapes=[
            pltpu.VMEM((N_GROUPS, N_STATE, GROUP_WIDTH), F32),
            pltpu.VMEM((tb, d_inner), F32),
            pltpu.VMEM((tb, LANES), F32),
            pltpu.VMEM((2 * N_GROUPS * 2, HALF_WIDTH, HALF_WIDTH), BF16),
        ],
        compiler_params=pltpu.CompilerParams(
            dimension_semantics=("arbitrary", "arbitrary"),
            vmem_limit_bytes=_vmem_limit(blocks)),
        name="ssd_scan",
    )(proj, proj, proj, dt, a_log, d_skip, norm_g)


def _outproj_kernel(a_ref, w_ref, x_ref, g_ref, *out_refs, final):
    acc = jnp.dot(a_ref[...], w_ref[...], preferred_element_type=F32)
    x1 = x_ref[...] + acc
    hn = _rmsnorm(x1, g_ref[...])
    if final:
        out_refs[0][...] = hn
    else:
        out_refs[0][...] = x1
        out_refs[1][...] = hn.astype(BF16)


def _outproj(a, w, x2, g, *, final, tm=512):
    m, k = a.shape
    d = w.shape[1]
    row_spec = pl.BlockSpec((tm, d), lambda i: (i, 0))
    if final:
        out_specs = [row_spec]
        out_shape = [jax.ShapeDtypeStruct((m, d), F32)]
    else:
        out_specs = [row_spec, row_spec]
        out_shape = [jax.ShapeDtypeStruct((m, d), F32), jax.ShapeDtypeStruct((m, d), BF16)]
    blocks = 2 * tm * k * 2 + k * d * 2 + 2 * tm * d * 4 + 2 * tm * d * 4 + 2 * tm * d * 2
    return pl.pallas_call(
        functools.partial(_outproj_kernel, final=final),
        grid=(m // tm,),
        in_specs=[
            pl.BlockSpec((tm, k), lambda i: (i, 0)),
            pl.BlockSpec((k, d), lambda i: (0, 0), pipeline_mode=pl.Buffered(1)),
            row_spec,
            pl.BlockSpec((1, d), lambda i: (0, 0)),
        ],
        out_specs=out_specs,
        out_shape=out_shape,
        compiler_params=pltpu.CompilerParams(
            dimension_semantics=("arbitrary",),
            vmem_limit_bytes=_vmem_limit(blocks)),
        name="outproj_final" if final else "outproj",
    )(a, w, x2, g)


def _pool_group_kernel(m_ref, w_ref, gate_ref, scale_ref, out_ref, wb_ref):
    @pl.when(pl.program_id(1) == 0)
    def _():
        wb_ref[...] = w_ref[...].astype(BF16)

    t = jnp.dot(m_ref[...], wb_ref[...], preferred_element_type=F32)
    out_ref[...] = (t * scale_ref[...] * _silu(gate_ref[...])).astype(out_ref.dtype)


def _pool_group(mixed, w_group, gate, scale, *, tm=1024):
    m, d_pool = mixed.shape
    n_groups, gd, _ = w_group.shape
    blocks = 2 * tm * gd * 2 + 2 * gd * gd * 4 + gd * gd * 2 + 2 * tm * gd * 4 + 2 * tm * gd * 2
    return pl.pallas_call(
        _pool_group_kernel,
        grid=(n_groups, m // tm),
        in_specs=[
            pl.BlockSpec((tm, gd), lambda g, i: (i, g)),
            pl.BlockSpec((None, gd, gd), lambda g, i: (g, 0, 0)),
            pl.BlockSpec((tm, gd), lambda g, i: (i, g)),
            pl.BlockSpec((1, gd), lambda g, i: (0, g)),
        ],
        out_specs=pl.BlockSpec((tm, gd), lambda g, i: (i, g)),
        out_shape=jax.ShapeDtypeStruct((m, d_pool), BF16),
        scratch_shapes=[pltpu.VMEM((gd, gd), BF16)],
        compiler_params=pltpu.CompilerParams(
            dimension_semantics=("arbitrary", "arbitrary"),
            vmem_limit_bytes=_vmem_limit(blocks)),
        name="pool_group",
    )(mixed, w_group, gate, scale)


def kernel(x, ln_g, final_g, ssm_w_in, ssm_conv_w, ssm_conv_b, ssm_dt_bias, ssm_a_log, ssm_d, ssm_norm_g, ssm_w_out, pool_w_in, pool_w_group, pool_scale, pool_w_out):
    batch, seq_len, d_model = x.shape
    m = batch * seq_len
    d_inner = ssm_w_out.shape[1]
    conv_dim = ssm_conv_w.shape[2]
    n_heads = ssm_dt_bias.shape[1]
    d_pool = pool_w_out.shape[1]
    assert d_inner == N_GROUPS * GROUP_WIDTH and conv_dim == d_inner + 2 * N_GROUPS * N_STATE
    assert n_heads == N_GROUPS * HEADS_PER_GROUP and n_heads <= LANES
    assert ssm_conv_w.shape[1] == CONV_TAPS and pool_w_group.shape[1] == len(POOL_WINDOWS)

    x2 = x.reshape(m, d_model)
    pad_heads = lambda v: jnp.pad(v, (0, LANES - n_heads)).reshape(1, LANES)

    w_in = ssm_w_in[0]
    n_main = d_inner + conv_dim
    w_dt = jnp.pad(w_in[:, n_main:], ((0, 0), (0, LANES - n_heads))).astype(BF16)
    proj, dt = _ssm_inproj(
        x2, ln_g[0].reshape(1, d_model), w_in.astype(BF16), w_dt,
        ssm_conv_w[0], ssm_conv_b[0].reshape(1, conv_dim), pad_heads(ssm_dt_bias[0]),
        seq_len=seq_len, d_inner=d_inner, conv_dim=conv_dim)
    d_skip = jnp.repeat(ssm_d[0], HEADDIM).reshape(1, d_inner)
    yn = _ssd_scan(proj, dt, pad_heads(ssm_a_log[0]), d_skip, ssm_norm_g[0].reshape(1, d_inner),
                   batch=batch, seq_len=seq_len, d_inner=d_inner)
    x1, hn1 = _outproj(yn, ssm_w_out[0].astype(BF16), x2, ln_g[1].reshape(1, d_model), final=False)

    mixed, gate = _pool_inproj(hn1, pool_w_in[0].astype(BF16), seq_len=seq_len, d_pool=d_pool)
    y1 = _pool_group(mixed, pool_w_group[0], gate, pool_scale[0].reshape(1, d_pool))
    (out,) = _outproj(y1, pool_w_out[0].astype(BF16), x1, final_g.reshape(1, d_model), final=True)
    return out.reshape(batch, seq_len, d_model)
```

```python
import functools

import jax
import jax.numpy as jnp
from jax import lax
from jax.experimental import pallas as pl
from jax.experimental.pallas import tpu as pltpu

F32 = jnp.float32
BF16 = jnp.bfloat16

EPS = 1e-6
LOG2_E = 1.4426950408889634
LANES = 128
SUBLANES = 8
V7X_VMEM_BYTES = 64 * 1024 * 1024

SSD_CHUNK = 64
HEADDIM = 64
N_STATE = 128
N_GROUPS = 8
HEADS_PER_GROUP = 8
GROUP_WIDTH = HEADS_PER_GROUP * HEADDIM
HALF_WIDTH = GROUP_WIDTH // 2
CONV_TAPS = 4
CONV_HALO = SUBLANES
POOL_WINDOWS = (2, 4, 8, 16)
POOL_HALO = 16
CHUNK_PAIR = 2 * SSD_CHUNK


def _vmem_limit(block_bytes):
    return int(min(block_bytes + 20 * 1024 * 1024, V7X_VMEM_BYTES - 4 * 1024 * 1024))


def _rmsnorm(x, g):
    ms = jnp.mean(x * x, axis=-1, keepdims=True)
    return x * lax.rsqrt(ms + EPS) * g


def _silu_of_twice(h):
    return h + h * jnp.tanh(h)


def _silu(x):
    return _silu_of_twice(0.5 * x)


def _softplus(x):
    return jnp.maximum(x, 0.0) + jnp.log1p(jnp.exp(-jnp.abs(x)))


def _row_groups(v):
    return v.reshape(v.shape[0] // SUBLANES, SUBLANES, v.shape[1])


def _shift_rows_down(g3, s, keep_first=False):
    sub = lax.broadcasted_iota(jnp.int32, (1, SUBLANES, g3.shape[2]), 1)
    moved = pltpu.roll(jnp.where(sub >= SUBLANES - s, g3[:-1], g3[1:]), s, 1)
    return jnp.concatenate([pltpu.roll(g3[:1], s, 1), moved], axis=0) if keep_first else moved


def _ssm_inproj_kernel(x_ref, g_ref, w_ref, wdt_ref, cw_ref, cb_ref, dtb_ref,
                       out_ref, dt_ref, hn_ref, carry_ref, taps_ref,
                       *, n_z_tiles, tiles_per_seq):
    i = pl.program_id(0)
    j = pl.program_id(1)
    tm = out_ref.shape[0]

    @pl.when(j == 0)
    def _():
        hn = _rmsnorm(x_ref[...], g_ref[...]).astype(BF16)
        hn_ref[...] = hn
        d = jnp.dot(hn, wdt_ref[...], preferred_element_type=F32)
        dt_ref[...] = _softplus(d + dtb_ref[...])

    @pl.when(j < n_z_tiles)
    def _():
        out_ref[...] = jnp.dot(hn_ref[...], w_ref[...], preferred_element_type=F32)

    @pl.when(j >= n_z_tiles)
    def _():
        jc = j - n_z_tiles
        tn = out_ref.shape[1]

        @pl.when(i % tiles_per_seq == 0)
        def _():
            carry_ref[jc] = jnp.zeros(carry_ref.shape[1:], F32)

        for k in range(CONV_TAPS):
            taps_ref[k] = jnp.broadcast_to(0.5 * cw_ref[k:k + 1, :], (SUBLANES, tn))
        taps_ref[CONV_TAPS] = jnp.broadcast_to(0.5 * cb_ref[...], (SUBLANES, tn))

        acc = jnp.dot(hn_ref[...], w_ref[...], preferred_element_type=F32)
        e = _row_groups(jnp.concatenate([carry_ref[jc], acc], axis=0))
        carry_ref[jc] = acc[tm - CONV_HALO:, :]
        h = taps_ref[CONV_TAPS]
        for k in range(CONV_TAPS):
            shift = CONV_TAPS - 1 - k
            h = h + (_shift_rows_down(e, shift) if shift else e[1:]) * taps_ref[k]
        out_ref[...] = _silu_of_twice(h).reshape(tm, tn)


def _ssm_inproj(x2, g, w, w_dt, conv_w, conv_b, dt_bias, *, seq_len, d_inner, conv_dim, tm=1024, tn=1024):
    m, d = x2.shape
    n_main = d_inner + conv_dim
    n_z_tiles = d_inner // tn
    dt_w = w_dt.shape[1]
    blocks = (2 * tm * d * 4 + tm * d * 2 + 2 * d * tn * 2 + 2 * d * dt_w * 2 + 2 * tm * tn * 4
              + 2 * tm * dt_w * 4)
    conv_idx = lambda i, j: (0, jnp.maximum(j - n_z_tiles, 0))
    return pl.pallas_call(
        functools.partial(_ssm_inproj_kernel, n_z_tiles=n_z_tiles, tiles_per_seq=seq_len // tm),
        grid=(m // tm, n_main // tn),
        in_specs=[
            pl.BlockSpec((tm, d), lambda i, j: (i, 0)),
            pl.BlockSpec((1, d), lambda i, j: (0, 0)),
            pl.BlockSpec((d, tn), lambda i, j: (0, j)),
            pl.BlockSpec((d, dt_w), lambda i, j: (0, 0)),
            pl.BlockSpec((CONV_TAPS, tn), conv_idx),
            pl.BlockSpec((1, tn), conv_idx),
            pl.BlockSpec((1, dt_w), lambda i, j: (0, 0)),
        ],
        out_specs=[
            pl.BlockSpec((tm, tn), lambda i, j: (i, j)),
            pl.BlockSpec((tm, dt_w), lambda i, j: (i, 0)),
        ],
        out_shape=[
            jax.ShapeDtypeStruct((m, n_main), F32),
            jax.ShapeDtypeStruct((m, dt_w), F32),
        ],
        scratch_shapes=[
            pltpu.VMEM((tm, d), BF16),
            pltpu.VMEM((conv_dim // tn, CONV_HALO, tn), F32),
            pltpu.VMEM((CONV_TAPS + 1, SUBLANES, tn), F32),
        ],
        compiler_params=pltpu.CompilerParams(
            dimension_semantics=("arbitrary", "arbitrary"),
            vmem_limit_bytes=_vmem_limit(blocks)),
        name="ssm_inproj",
    )(x2, g, w, w_dt, conv_w, conv_b, dt_bias)


def _pool_inproj_kernel(h_ref, w_ref, mixed_ref, gate_ref, hs_ref, carry_ref,
                        *, n_u_tiles, tiles_per_seq, tiles_per_group):
    i = pl.program_id(0)
    j = pl.program_id(1)
    tm, tn = mixed_ref.shape

    @pl.when(j == 0)
    def _():
        hs_ref[...] = h_ref[...]

    @pl.when(j >= n_u_tiles)
    def _():
        gate_ref[...] = jnp.dot(hs_ref[...], w_ref[...], preferred_element_type=F32).astype(gate_ref.dtype)

    @pl.when(j < n_u_tiles)
    def _():
        @pl.when(i % tiles_per_seq == 0)
        def _():
            carry_ref[j] = jnp.zeros(carry_ref.shape[1:], F32)

        pos0 = (i % tiles_per_seq) * tm + 1

        for k, win in enumerate(POOL_WINDOWS):
            @pl.when(j // tiles_per_group == k)
            def _(win=win):
                u = jnp.dot(hs_ref[...], w_ref[...], preferred_element_type=F32)
                e = _row_groups(jnp.concatenate([carry_ref[j], u], axis=0))
                carry_ref[j] = u[tm - POOL_HALO:, :]
                acc = e[1:] + e[:-1] if win > SUBLANES else e[1:]
                span = 1
                while span < min(win, SUBLANES):
                    acc = acc + _shift_rows_down(acc, span, keep_first=True)
                    span *= 2
                shape = (tm // SUBLANES, SUBLANES, tn)
                pos = (pos0 + SUBLANES * lax.broadcasted_iota(jnp.int32, shape, 0)
                       + lax.broadcasted_iota(jnp.int32, shape, 1))
                cnt = jnp.minimum(pos, win).astype(F32)
                mixed_ref[...] = (acc[1:] / cnt - e[2:]).reshape(tm, tn).astype(mixed_ref.dtype)


def _pool_inproj(h, w, *, seq_len, d_pool, tm=1024, tn=1024):
    m, d = h.shape
    n_u_tiles = d_pool // tn
    assert POOL_WINDOWS == tuple(2 << k for k in range(len(POOL_WINDOWS)))
    blocks = 2 * tm * d * 2 + tm * d * 2 + 2 * d * tn * 2 + 2 * 2 * tm * tn * 2
    return pl.pallas_call(
        functools.partial(_pool_inproj_kernel, n_u_tiles=n_u_tiles, tiles_per_seq=seq_len // tm,
                          tiles_per_group=d_pool // len(POOL_WINDOWS) // tn),
        grid=(m // tm, 2 * n_u_tiles),
        in_specs=[
            pl.BlockSpec((tm, d), lambda i, j: (i, 0)),
            pl.BlockSpec((d, tn), lambda i, j: (0, j)),
        ],
        out_specs=[
            pl.BlockSpec((tm, tn), lambda i, j: (i, jnp.minimum(j, n_u_tiles - 1))),
            pl.BlockSpec((tm, tn), lambda i, j: (i, jnp.maximum(j - n_u_tiles, 0))),
        ],
        out_shape=[
            jax.ShapeDtypeStruct((m, d_pool), BF16),
            jax.ShapeDtypeStruct((m, d_pool), BF16),
        ],
        scratch_shapes=[
            pltpu.VMEM((tm, d), BF16),
            pltpu.VMEM((n_u_tiles, POOL_HALO, tn), F32),
        ],
        compiler_params=pltpu.CompilerParams(
            dimension_semantics=("arbitrary", "arbitrary"),
            vmem_limit_bytes=_vmem_limit(blocks)),
        name="pool_inproj",
    )(h, w)


def _split3(x):
    hi = x.astype(BF16)
    r1 = x - hi.astype(F32)
    mid = r1.astype(BF16)
    lo = (r1 - mid.astype(F32)).astype(BF16)
    return hi, mid, lo


def _ssd_kernel(z_ref, xs_ref, bc_ref, dt_ref, alog_ref, dskip_ref, ng_ref,
                out_ref, state_ref, y_ref, ssq_ref, xdiag_ref):
    tb, d_inner = z_ref.shape
    q = SSD_CHUNK

    @pl.when(pl.program_id(1) == 0)
    def _():
        state_ref[...] = jnp.zeros_like(state_ref)

    @pl.when((pl.program_id(0) == 0) & (pl.program_id(1) == 0))
    def _():
        xdiag_ref[...] = jnp.zeros_like(xdiag_ref)

    a = -jnp.exp(alog_ref[...]) * LOG2_E

    r2 = lax.broadcasted_iota(jnp.int32, (CHUNK_PAIR, CHUNK_PAIR), 0)
    c2 = lax.broadcasted_iota(jnp.int32, (CHUNK_PAIR, CHUNK_PAIR), 1)
    tril_pair = jnp.where((r2 // q == c2 // q) & (r2 >= c2), 1.0, 0.0).astype(BF16)
    second_chunk = r2 >= q

    lrow = lax.broadcasted_iota(jnp.int32, (q, LANES), 0)
    lane = lax.broadcasted_iota(jnp.int32, (q, LANES), 1)
    lane_hi_i = (lane >= q).astype(jnp.int32)
    causal_pair = lrow >= (lane % q)
    lane_hi_row = lax.broadcasted_iota(jnp.int32, (1, LANES), 1) >= q

    def pair_body(p, carry):
        r0 = pl.multiple_of(p * CHUNK_PAIR, CHUNK_PAIR)
        dt2 = dt_ref[pl.ds(r0, CHUNK_PAIR), :]
        hi, mid, lo = _split3(dt2 * a)
        cs = (jnp.dot(tril_pair, hi, preferred_element_type=F32)
              + jnp.dot(tril_pair, mid, preferred_element_type=F32)
              + jnp.dot(tril_pair, lo, preferred_element_type=F32))
        last = jnp.where(second_chunk, cs[CHUNK_PAIR - 1:CHUNK_PAIR, :], cs[q - 1:q, :])
        wdt = jnp.exp2(last - cs) * dt2
        cs_t, dt_t, wd_t = cs.T, dt2.T, wdt.T
        cs_r, dt_r, wd_r = (pltpu.roll(v, q, 1) for v in (cs_t, dt_t, wd_t))

        for c in range(2):
            rows = pl.ds(r0 + c * q, q)
            cs_c = cs[c * q:(c + 1) * q]
            pick = (lambda t, r: (t, r)) if c == 0 else (lambda t, r: (r, t))
            (cs_lo, cs_hi), (dt_lo, dt_hi), (wd_lo, wd_hi) = pick(cs_t, cs_r), pick(dt_t, dt_r), pick(wd_t, wd_r)

            def pair_row(lo_src, hi_src, ha):
                return jnp.where(lane_hi_row, hi_src[ha + 1:ha + 2, :], lo_src[ha:ha + 1, :])

            def early(g):
                cols_g = slice(g * GROUP_WIDTH, (g + 1) * GROUP_WIDTH)
                bm32 = bc_ref[rows, g * N_STATE:(g + 1) * N_STATE]
                bm2 = jnp.concatenate([bm32, bm32], axis=0)
                cm = bc_ref[rows, (N_GROUPS + g) * N_STATE:(N_GROUPS + g + 1) * N_STATE].astype(BF16)
                cb2 = lax.dot_general(cm, bm2.astype(BF16), (((1,), (1,)), ((), ())),
                                      preferred_element_type=F32)
                yoff = jnp.dot(cm, state_ref[g].astype(BF16), preferred_element_type=F32)
                xsb = xs_ref[rows, cols_g].astype(BF16)
                for k in range(2):
                    for r in range(HALF_WIDTH // HEADDIM):
                        blk = slice(r * HEADDIM, (r + 1) * HEADDIM)
                        xdiag_ref[(c * N_GROUPS + g) * 2 + k, blk, blk] = xsb[
                            :, k * HALF_WIDTH + r * HEADDIM:k * HALF_WIDTH + (r + 1) * HEADDIM]
                return cb2, bm2.T, yoff

            def late(g, cb2, bt2, yoff, ssq):
                h0 = g * HEADS_PER_GROUP
                cols_g = slice(g * GROUP_WIDTH, (g + 1) * GROUP_WIDTH)
                top, bot, cols = [], [], []
                for p2 in range(HEADS_PER_GROUP // 2):
                    ha = h0 + 2 * p2
                    col = jnp.take_along_axis(cs_c, lane_hi_i + ha, axis=1, mode="promise_in_bounds")
                    decay = jnp.exp2(jnp.where(causal_pair, col - pair_row(cs_lo, cs_hi, ha), -jnp.inf))
                    top.append((cb2 * decay * pair_row(dt_lo, dt_hi, ha)).astype(BF16))
                    bot.append((bt2 * pair_row(wd_lo, wd_hi, ha)).astype(BF16))
                    cols.append(col)
                ecs = jnp.exp2(jnp.concatenate(cols, axis=1))
                ydiag, upd = [], []
                for k in range(2):
                    lhs = jnp.concatenate([jnp.concatenate(top[2 * k:2 * k + 2], axis=1),
                                           jnp.concatenate(bot[2 * k:2 * k + 2], axis=1)], axis=0)
                    o = jnp.dot(lhs, xdiag_ref[(c * N_GROUPS + g) * 2 + k],
                                preferred_element_type=F32)
                    ydiag.append(o[:q])
                    upd.append(o[q:])
                y = jnp.concatenate(ydiag, axis=1) + yoff * ecs
                y = y + dskip_ref[:, cols_g] * xs_ref[rows, cols_g]
                yg = y * _silu(z_ref[rows, cols_g])
                y_ref[rows, cols_g] = yg
                sq = yg * yg
                for t in range(GROUP_WIDTH // LANES):
                    ssq = ssq + sq[:, t * LANES:(t + 1) * LANES]
                state_ref[g] = state_ref[g] * ecs[q - 1:q, :] + jnp.concatenate(upd, axis=1)
                return ssq

            ssq = jnp.zeros((q, LANES), F32)
            staged = early(0)
            for g in range(N_GROUPS):
                current = staged
                if g + 1 < N_GROUPS:
                    staged = early(g + 1)
                ssq = late(g, *current, ssq)
            ssq_ref[rows, :] = ssq
        return carry

    lax.fori_loop(0, tb // CHUNK_PAIR, pair_body, 0)

    ms = jnp.sum(ssq_ref[...], axis=-1, keepdims=True) * (1.0 / d_inner)
    ssq_ref[...] = jnp.broadcast_to(lax.rsqrt(ms + EPS), ssq_ref.shape)

    slab = 4 * SUBLANES

    def norm_body(s, carry):
        rr = pl.ds(pl.multiple_of(s * slab, slab), slab)
        inv_rms = jnp.tile(ssq_ref[rr, :], (1, d_inner // LANES))
        out_ref[rr, :] = (y_ref[rr, :] * inv_rms * ng_ref[...]).astype(BF16)
        return carry

    lax.fori_loop(0, tb // slab, norm_body, 0)


def _ssd_scan(proj, dt, a_log, d_skip, norm_g, *, batch, seq_len, d_inner, tb=256):
    m = proj.shape[0]
    bc_w = 2 * N_GROUPS * N_STATE
    n_l = seq_len // tb
    row = lambda b, l: b * n_l + l
    blocks = (2 * 2 * tb * d_inner * 4 + 2 * tb * bc_w * 4 + 2 * tb * LANES * 4 + 2 * tb * d_inner * 2
              + N_GROUPS * N_STATE * GROUP_WIDTH * 4 + tb * d_inner * 4 + tb * LANES * 4
              + 4 * N_GROUPS * HALF_WIDTH * HALF_WIDTH * 2)
    return pl.pallas_call(
        _ssd_kernel,
        grid=(batch, n_l),
        in_specs=[
            pl.BlockSpec((tb, d_inner), lambda b, l: (row(b, l), 0)),
            pl.BlockSpec((tb, d_inner), lambda b, l: (row(b, l), 1)),
            pl.BlockSpec((tb, bc_w), lambda b, l: (row(b, l), 2 * d_inner // bc_w)),
            pl.BlockSpec((tb, LANES), lambda b, l: (row(b, l), 0)),
            pl.BlockSpec((1, LANES), lambda b, l: (0, 0)),
            pl.BlockSpec((1, d_inner), lambda b, l: (0, 0)),
            pl.BlockSpec((1, d_inner), lambda b, l: (0, 0)),
        ],
        out_specs=pl.BlockSpec((tb, d_inner), lambda b, l: (row(b, l), 0)),
        out_shape=jax.ShapeDtypeStruct((m, d_inner), BF16),
        scratch_shapes=[
            pltpu.VMEM((N_GROUPS, N_STATE, GROUP_WIDTH), F32),
            pltpu.VMEM((tb, d_inner), F32),
            pltpu.VMEM((tb, LANES), F32),
            pltpu.VMEM((2 * N_GROUPS * 2, HALF_WIDTH, HALF_WIDTH), BF16),
        ],
        compiler_params=pltpu.CompilerParams(
            dimension_semantics=("arbitrary", "arbitrary"),
            vmem_limit_bytes=_vmem_limit(blocks)),
        name="ssd_scan",
    )(proj, proj, proj, dt, a_log, d_skip, norm_g)


def _outproj_kernel(a_ref, w_ref, x_ref, g_ref, *out_refs, final):
    acc = jnp.dot(a_ref[...], w_ref[...], preferred_element_type=F32)
    x1 = x_ref[...] + acc
    hn = _rmsnorm(x1, g_ref[...])
    if final:
        out_refs[0][...] = hn
    else:
        out_refs[0][...] = x1
        out_refs[1][...] = hn.astype(BF16)


def _outproj(a, w, x2, g, *, final, tm=512):
    m, k = a.shape
    d = w.shape[1]
    row_spec = pl.BlockSpec((tm, d), lambda i: (i, 0))
    if final:
        out_specs = [row_spec]
        out_shape = [jax.ShapeDtypeStruct((m, d), F32)]
    else:
        out_specs = [row_spec, row_spec]
        out_shape = [jax.ShapeDtypeStruct((m, d), F32), jax.ShapeDtypeStruct((m, d), BF16)]
    blocks = 2 * tm * k * 2 + k * d * 2 + 2 * tm * d * 4 + 2 * tm * d * 4 + 2 * tm * d * 2
    return pl.pallas_call(
        functools.partial(_outproj_kernel, final=final),
        grid=(m // tm,),
        in_specs=[
            pl.BlockSpec((tm, k), lambda i: (i, 0)),
            pl.BlockSpec((k, d), lambda i: (0, 0), pipeline_mode=pl.Buffered(1)),
            row_spec,
            pl.BlockSpec((1, d), lambda i: (0, 0)),
        ],
        out_specs=out_specs,
        out_shape=out_shape,
        compiler_params=pltpu.CompilerParams(
            dimension_semantics=("arbitrary",),
            vmem_limit_bytes=_vmem_limit(blocks)),
        name="outproj_final" if final else "outproj",
    )(a, w, x2, g)


def _pool_group_kernel(m_ref, w_ref, gate_ref, scale_ref, out_ref, wb_ref):
    @pl.when(pl.program_id(1) == 0)
    def _():
        wb_ref[...] = w_ref[...].astype(BF16)

    t = jnp.dot(m_ref[...], wb_ref[...], preferred_element_type=F32)
    out_ref[...] = (t * scale_ref[...] * _silu(gate_ref[...].astype(F32))).astype(out_ref.dtype)


def _pool_group(mixed, w_group, gate, scale, *, tm=1024):
    m, d_pool = mixed.shape
    n_groups, gd, _ = w_group.shape
    blocks = 3 * 2 * tm * gd * 2 + 2 * gd * gd * 4 + gd * gd * 2
    return pl.pallas_call(
        _pool_group_kernel,
        grid=(n_groups, m // tm),
        in_specs=[
            pl.BlockSpec((tm, gd), lambda g, i: (i, g)),
            pl.BlockSpec((None, gd, gd), lambda g, i: (g, 0, 0)),
            pl.BlockSpec((tm, gd), lambda g, i: (i, g)),
            pl.BlockSpec((1, gd), lambda g, i: (0, g)),
        ],
        out_specs=pl.BlockSpec((tm, gd), lambda g, i: (i, g)),
        out_shape=jax.ShapeDtypeStruct((m, d_pool), BF16),
        scratch_shapes=[pltpu.VMEM((gd, gd), BF16)],
        compiler_params=pltpu.CompilerParams(
            dimension_semantics=("arbitrary", "arbitrary"),
            vmem_limit_bytes=_vmem_limit(blocks)),
        name="pool_group",
    )(mixed, w_group, gate, scale)


def kernel(x, ln_g, final_g, ssm_w_in, ssm_conv_w, ssm_conv_b, ssm_dt_bias, ssm_a_log, ssm_d, ssm_norm_g, ssm_w_out, pool_w_in, pool_w_group, pool_scale, pool_w_out):
    batch, seq_len, d_model = x.shape
    m = batch * seq_len
    d_inner = ssm_w_out.shape[1]
    conv_dim = ssm_conv_w.shape[2]
    n_heads = ssm_dt_bias.shape[1]
    d_pool = pool_w_out.shape[1]
    assert d_inner == N_GROUPS * GROUP_WIDTH and conv_dim == d_inner + 2 * N_GROUPS * N_STATE
    assert n_heads == N_GROUPS * HEADS_PER_GROUP and n_heads <= LANES
    assert ssm_conv_w.shape[1] == CONV_TAPS and pool_w_group.shape[1] == len(POOL_WINDOWS)

    x2 = x.reshape(m, d_model)
    pad_heads = lambda v: jnp.pad(v, (0, LANES - n_heads)).reshape(1, LANES)

    w_in = ssm_w_in[0]
    n_main = d_inner + conv_dim
    w_dt = jnp.pad(w_in[:, n_main:], ((0, 0), (0, LANES - n_heads))).astype(BF16)
    proj, dt = _ssm_inproj(
        x2, ln_g[0].reshape(1, d_model), w_in.astype(BF16), w_dt,
        ssm_conv_w[0], ssm_conv_b[0].reshape(1, conv_dim), pad_heads(ssm_dt_bias[0]),
        seq_len=seq_len, d_inner=d_inner, conv_dim=conv_dim)
    d_skip = jnp.repeat(ssm_d[0], HEADDIM).reshape(1, d_inner)
    yn = _ssd_scan(proj, dt, pad_heads(ssm_a_log[0]), d_skip, ssm_norm_g[0].reshape(1, d_inner),
                   batch=batch, seq_len=seq_len, d_inner=d_inner)
    x1, hn1 = _outproj(yn, ssm_w_out[0].astype(BF16), x2, ln_g[1].reshape(1, d_model), final=False)

    mixed, gate = _pool_inproj(hn1, pool_w_in[0].astype(BF16), seq_len=seq_len, d_pool=d_pool)
    y1 = _pool_group(mixed, pool_w_group[0], gate, pool_scale[0].reshape(1, d_pool))
    (out,) = _outproj(y1, pool_w_out[0].astype(BF16), x1, final_g.reshape(1, d_model), final=True)
    return out.reshape(batch, seq_len, d_model)
```

```python
import functools

import jax
import jax.numpy as jnp
from jax import lax
from jax.experimental import pallas as pl
from jax.experimental.pallas import tpu as pltpu

F32 = jnp.float32
BF16 = jnp.bfloat16

EPS = 1e-6
LOG2_E = 1.4426950408889634
LANES = 128
SUBLANES = 8
V7X_VMEM_BYTES = 64 * 1024 * 1024

SSD_CHUNK = 64
HEADDIM = 64
N_STATE = 128
N_GROUPS = 8
HEADS_PER_GROUP = 8
GROUP_WIDTH = HEADS_PER_GROUP * HEADDIM
HALF_WIDTH = GROUP_WIDTH // 2
CONV_TAPS = 4
CONV_HALO = SUBLANES
POOL_WINDOWS = (2, 4, 8, 16)
POOL_HALO = 16
CHUNK_PAIR = 2 * SSD_CHUNK
LATER_WEIGHT_BLOCKS = 128


def _vmem_limit(block_bytes):
    return int(min(block_bytes + 20 * 1024 * 1024, V7X_VMEM_BYTES - 4 * 1024 * 1024))


def _rmsnorm(x, g):
    ms = jnp.mean(x * x, axis=-1, keepdims=True)
    return x * lax.rsqrt(ms + EPS) * g


def _silu_of_twice(h):
    return h + h * jnp.tanh(h)


def _silu(x):
    return _silu_of_twice(0.5 * x)


def _softplus(x):
    return jnp.maximum(x, 0.0) + jnp.log1p(jnp.exp(-jnp.abs(x)))


def _row_groups(v):
    return v.reshape(v.shape[0] // SUBLANES, SUBLANES, v.shape[1])


def _shift_rows_down(g3, s, keep_first=False):
    sub = lax.broadcasted_iota(jnp.int32, (1, SUBLANES, g3.shape[2]), 1)
    moved = pltpu.roll(jnp.where(sub >= SUBLANES - s, g3[:-1], g3[1:]), s, 1)
    return jnp.concatenate([pltpu.roll(g3[:1], s, 1), moved], axis=0) if keep_first else moved


def _ssm_inproj_kernel(x_ref, g_ref, w_ref, wdt_ref, cw_ref, cb_ref, dtb_ref, *refs,
                       n_z_tiles, tiles_per_seq, n_later):
    later_f32 = refs[:n_later]
    out_ref, dt_ref = refs[n_later:n_later + 2]
    later_bf16 = refs[n_later + 2:2 * n_later + 2]
    hn_ref, carry_ref, taps_ref = refs[2 * n_later + 2:]
    i = pl.program_id(0)
    j = pl.program_id(1)
    tm = out_ref.shape[0]

    for src_ref, dst_ref in zip(later_f32, later_bf16):
        dst_ref[...] = src_ref[...].astype(dst_ref.dtype)

    @pl.when(j == 0)
    def _():
        hn = _rmsnorm(x_ref[...], g_ref[...]).astype(BF16)
        hn_ref[...] = hn
        d = jnp.dot(hn, wdt_ref[...], preferred_element_type=F32)
        dt_ref[...] = _softplus(d + dtb_ref[...])

    @pl.when(j < n_z_tiles)
    def _():
        out_ref[...] = jnp.dot(hn_ref[...], w_ref[...], preferred_element_type=F32)

    @pl.when(j >= n_z_tiles)
    def _():
        jc = j - n_z_tiles
        tn = out_ref.shape[1]

        @pl.when(i % tiles_per_seq == 0)
        def _():
            carry_ref[jc] = jnp.zeros(carry_ref.shape[1:], F32)

        for k in range(CONV_TAPS):
            taps_ref[k] = jnp.broadcast_to(0.5 * cw_ref[k:k + 1, :], (SUBLANES, tn))
        taps_ref[CONV_TAPS] = jnp.broadcast_to(0.5 * cb_ref[...], (SUBLANES, tn))

        acc = jnp.dot(hn_ref[...], w_ref[...], preferred_element_type=F32)
        e = _row_groups(jnp.concatenate([carry_ref[jc], acc], axis=0))
        carry_ref[jc] = acc[tm - CONV_HALO:, :]
        h = taps_ref[CONV_TAPS]
        for k in range(CONV_TAPS):
            shift = CONV_TAPS - 1 - k
            h = h + (_shift_rows_down(e, shift) if shift else e[1:]) * taps_ref[k]
        out_ref[...] = _silu_of_twice(h).reshape(tm, tn)


def _ssm_inproj(x2, g, w, w_dt, conv_w, conv_b, dt_bias, later_weights, *, seq_len, d_inner, conv_dim,
                tm=1024, tn=1024):
    m, d = x2.shape
    n_main = d_inner + conv_dim
    n_z_tiles = d_inner // tn
    n_cols = n_main // tn
    dt_w = w_dt.shape[1]
    later_specs = []
    for lw in later_weights:
        rows = max(2 * SUBLANES, lw.shape[0] // LATER_WEIGHT_BLOCKS)
        n_blocks = lw.shape[0] // rows
        assert n_blocks <= (m // tm) * n_cols and lw.shape[0] % rows == 0
        later_specs.append(pl.BlockSpec(
            (rows, lw.shape[1]), lambda i, j, n_blocks=n_blocks: (jnp.minimum(i * n_cols + j, n_blocks - 1), 0)))
    blocks = (2 * tm * d * 4 + tm * d * 2 + 2 * d * tn * 2 + 2 * d * dt_w * 2 + 2 * tm * tn * 4
              + 2 * tm * dt_w * 4)
    conv_idx = lambda i, j: (0, jnp.maximum(j - n_z_tiles, 0))
    return pl.pallas_call(
        functools.partial(_ssm_inproj_kernel, n_z_tiles=n_z_tiles, tiles_per_seq=seq_len // tm,
                          n_later=len(later_weights)),
        grid=(m // tm, n_cols),
        in_specs=[
            pl.BlockSpec((tm, d), lambda i, j: (i, 0)),
            pl.BlockSpec((1, d), lambda i, j: (0, 0)),
            pl.BlockSpec((d, tn), lambda i, j: (0, j)),
            pl.BlockSpec((d, dt_w), lambda i, j: (0, 0)),
            pl.BlockSpec((CONV_TAPS, tn), conv_idx),
            pl.BlockSpec((1, tn), conv_idx),
            pl.BlockSpec((1, dt_w), lambda i, j: (0, 0)),
        ] + later_specs,
        out_specs=[
            pl.BlockSpec((tm, tn), lambda i, j: (i, j)),
            pl.BlockSpec((tm, dt_w), lambda i, j: (i, 0)),
        ] + later_specs,
        out_shape=[
            jax.ShapeDtypeStruct((m, n_main), F32),
            jax.ShapeDtypeStruct((m, dt_w), F32),
        ] + [jax.ShapeDtypeStruct(lw.shape, BF16) for lw in later_weights],
        scratch_shapes=[
            pltpu.VMEM((tm, d), BF16),
            pltpu.VMEM((conv_dim // tn, CONV_HALO, tn), F32),
            pltpu.VMEM((CONV_TAPS + 1, SUBLANES, tn), F32),
        ],
        compiler_params=pltpu.CompilerParams(
            dimension_semantics=("arbitrary", "arbitrary"),
            vmem_limit_bytes=_vmem_limit(blocks)),
        name="ssm_inproj",
    )(x2, g, w, w_dt, conv_w, conv_b, dt_bias, *later_weights)


def _pool_inproj_kernel(h_ref, w_ref, mixed_ref, gate_ref, hs_ref, carry_ref,
                        *, n_u_tiles, tiles_per_seq, tiles_per_group):
    i = pl.program_id(0)
    j = pl.program_id(1)
    tm, tn = mixed_ref.shape

    @pl.when(j == 0)
    def _():
        hs_ref[...] = h_ref[...]

    @pl.when(j >= n_u_tiles)
    def _():
        gate_ref[...] = jnp.dot(hs_ref[...], w_ref[...], preferred_element_type=F32).astype(gate_ref.dtype)

    @pl.when(j < n_u_tiles)
    def _():
        @pl.when(i % tiles_per_seq == 0)
        def _():
            carry_ref[j] = jnp.zeros(carry_ref.shape[1:], F32)

        pos0 = (i % tiles_per_seq) * tm + 1

        for k, win in enumerate(POOL_WINDOWS):
            @pl.when(j // tiles_per_group == k)
            def _(win=win):
                u = jnp.dot(hs_ref[...], w_ref[...], preferred_element_type=F32)
                e = _row_groups(jnp.concatenate([carry_ref[j], u], axis=0))
                carry_ref[j] = u[tm - POOL_HALO:, :]
                acc = e[1:] + e[:-1] if win > SUBLANES else e[1:]
                span = 1
                while span < min(win, SUBLANES):
                    acc = acc + _shift_rows_down(acc, span, keep_first=True)
                    span *= 2
                shape = (tm // SUBLANES, SUBLANES, tn)
                pos = (pos0 + SUBLANES * lax.broadcasted_iota(jnp.int32, shape, 0)
                       + lax.broadcasted_iota(jnp.int32, shape, 1))
                cnt = jnp.minimum(pos, win).astype(F32)
                mixed_ref[...] = (acc[1:] / cnt - e[2:]).reshape(tm, tn).astype(mixed_ref.dtype)


def _pool_inproj(h, w, *, seq_len, d_pool, tm=1024, tn=1024):
    m, d = h.shape
    n_u_tiles = d_pool // tn
    assert POOL_WINDOWS == tuple(2 << k for k in range(len(POOL_WINDOWS)))
    blocks = 2 * tm * d * 2 + tm * d * 2 + 2 * d * tn * 2 + 2 * 2 * tm * tn * 2
    return pl.pallas_call(
        functools.partial(_pool_inproj_kernel, n_u_tiles=n_u_tiles, tiles_per_seq=seq_len // tm,
                          tiles_per_group=d_pool // len(POOL_WINDOWS) // tn),
        grid=(m // tm, 2 * n_u_tiles),
        in_specs=[
            pl.BlockSpec((tm, d), lambda i, j: (i, 0)),
            pl.BlockSpec((d, tn), lambda i, j: (0, j)),
        ],
        out_specs=[
            pl.BlockSpec((tm, tn), lambda i, j: (i, jnp.minimum(j, n_u_tiles - 1))),
            pl.BlockSpec((tm, tn), lambda i, j: (i, jnp.maximum(j - n_u_tiles, 0))),
        ],
        out_shape=[
            jax.ShapeDtypeStruct((m, d_pool), BF16),
            jax.ShapeDtypeStruct((m, d_pool), BF16),
        ],
        scratch_shapes=[
            pltpu.VMEM((tm, d), BF16),
            pltpu.VMEM((n_u_tiles, POOL_HALO, tn), F32),
        ],
        compiler_params=pltpu.CompilerParams(
            dimension_semantics=("arbitrary", "arbitrary"),
            vmem_limit_bytes=_vmem_limit(blocks)),
        name="pool_inproj",
    )(h, w)


def _split3(x):
    hi = x.astype(BF16)
    r1 = x - hi.astype(F32)
    mid = r1.astype(BF16)
    lo = (r1 - mid.astype(F32)).astype(BF16)
    return hi, mid, lo


def _ssd_kernel(z_ref, xs_ref, bc_ref, dt_ref, alog_ref, dskip_ref, ng_ref,
                out_ref, state_ref, y_ref, ssq_ref, xdiag_ref):
    tb, d_inner = z_ref.shape
    q = SSD_CHUNK

    @pl.when(pl.program_id(1) == 0)
    def _():
        state_ref[...] = jnp.zeros_like(state_ref)

    @pl.when((pl.program_id(0) == 0) & (pl.program_id(1) == 0))
    def _():
        xdiag_ref[...] = jnp.zeros_like(xdiag_ref)

    a = -jnp.exp(alog_ref[...]) * LOG2_E

    r2 = lax.broadcasted_iota(jnp.int32, (CHUNK_PAIR, CHUNK_PAIR), 0)
    c2 = lax.broadcasted_iota(jnp.int32, (CHUNK_PAIR, CHUNK_PAIR), 1)
    tril_pair = jnp.where((r2 // q == c2 // q) & (r2 >= c2), 1.0, 0.0).astype(BF16)
    second_chunk = r2 >= q

    lrow = lax.broadcasted_iota(jnp.int32, (q, LANES), 0)
    lane = lax.broadcasted_iota(jnp.int32, (q, LANES), 1)
    lane_hi_i = (lane >= q).astype(jnp.int32)
    causal_pair = lrow >= (lane % q)
    lane_hi_row = lax.broadcasted_iota(jnp.int32, (1, LANES), 1) >= q

    def pair_body(p, carry):
        r0 = pl.multiple_of(p * CHUNK_PAIR, CHUNK_PAIR)
        dt2 = dt_ref[pl.ds(r0, CHUNK_PAIR), :]
        hi, mid, lo = _split3(dt2 * a)
        cs = (jnp.dot(tril_pair, hi, preferred_element_type=F32)
              + jnp.dot(tril_pair, mid, preferred_element_type=F32)
              + jnp.dot(tril_pair, lo, preferred_element_type=F32))
        last = jnp.where(second_chunk, cs[CHUNK_PAIR - 1:CHUNK_PAIR, :], cs[q - 1:q, :])
        wdt = jnp.exp2(last - cs) * dt2
        cs_t, dt_t, wd_t = cs.T, dt2.T, wdt.T
        cs_r, dt_r, wd_r = (pltpu.roll(v, q, 1) for v in (cs_t, dt_t, wd_t))

        for c in range(2):
            rows = pl.ds(r0 + c * q, q)
            cs_c = cs[c * q:(c + 1) * q]
            pick = (lambda t, r: (t, r)) if c == 0 else (lambda t, r: (r, t))
            (cs_lo, cs_hi), (dt_lo, dt_hi), (wd_lo, wd_hi) = pick(cs_t, cs_r), pick(dt_t, dt_r), pick(wd_t, wd_r)

            def pair_row(lo_src, hi_src, ha):
                return jnp.where(lane_hi_row, hi_src[ha + 1:ha + 2, :], lo_src[ha:ha + 1, :])

            def early(g):
                cols_g = slice(g * GROUP_WIDTH, (g + 1) * GROUP_WIDTH)
                bm32 = bc_ref[rows, g * N_STATE:(g + 1) * N_STATE]
                bm2 = jnp.concatenate([bm32, bm32], axis=0)
                cm = bc_ref[rows, (N_GROUPS + g) * N_STATE:(N_GROUPS + g + 1) * N_STATE].astype(BF16)
                cb2 = lax.dot_general(cm, bm2.astype(BF16), (((1,), (1,)), ((), ())),
                                      preferred_element_type=F32)
                yoff = jnp.dot(cm, state_ref[g].astype(BF16), preferred_element_type=F32)
                xsb = xs_ref[rows, cols_g].astype(BF16)
                for k in range(2):
                    for r in range(HALF_WIDTH // HEADDIM):
                        blk = slice(r * HEADDIM, (r + 1) * HEADDIM)
                        xdiag_ref[(c * N_GROUPS + g) * 2 + k, blk, blk] = xsb[
                            :, k * HALF_WIDTH + r * HEADDIM:k * HALF_WIDTH + (r + 1) * HEADDIM]
                return cb2, bm2.T, yoff

            def late(g, cb2, bt2, yoff, ssq):
                h0 = g * HEADS_PER_GROUP
                cols_g = slice(g * GROUP_WIDTH, (g + 1) * GROUP_WIDTH)
                top, bot, cols = [], [], []
                for p2 in range(HEADS_PER_GROUP // 2):
                    ha = h0 + 2 * p2
                    col = jnp.take_along_axis(cs_c, lane_hi_i + ha, axis=1, mode="promise_in_bounds")
                    decay = jnp.exp2(jnp.where(causal_pair, col - pair_row(cs_lo, cs_hi, ha), -jnp.inf))
                    top.append((cb2 * decay * pair_row(dt_lo, dt_hi, ha)).astype(BF16))
                    bot.append((bt2 * pair_row(wd_lo, wd_hi, ha)).astype(BF16))
                    cols.append(col)
                ecs = jnp.exp2(jnp.concatenate(cols, axis=1))
                ydiag, upd = [], []
                for k in range(2):
                    lhs = jnp.concatenate([jnp.concatenate(top[2 * k:2 * k + 2], axis=1),
                                           jnp.concatenate(bot[2 * k:2 * k + 2], axis=1)], axis=0)
                    o = jnp.dot(lhs, xdiag_ref[(c * N_GROUPS + g) * 2 + k],
                                preferred_element_type=F32)
                    ydiag.append(o[:q])
                    upd.append(o[q:])
                y = jnp.concatenate(ydiag, axis=1) + yoff * ecs
                y = y + dskip_ref[:, cols_g] * xs_ref[rows, cols_g]
                yg = y * _silu(z_ref[rows, cols_g])
                y_ref[rows, cols_g] = yg
                sq = yg * yg
                for t in range(GROUP_WIDTH // LANES):
                    ssq = ssq + sq[:, t * LANES:(t + 1) * LANES]
                state_ref[g] = state_ref[g] * ecs[q - 1:q, :] + jnp.concatenate(upd, axis=1)
                return ssq

            ssq = jnp.zeros((q, LANES), F32)
            staged = early(0)
            for g in range(N_GROUPS):
                current = staged
                if g + 1 < N_GROUPS:
                    staged = early(g + 1)
                ssq = late(g, *current, ssq)
            ssq_ref[rows, :] = ssq
        return carry

    lax.fori_loop(0, tb // CHUNK_PAIR, pair_body, 0)

    ms = jnp.sum(ssq_ref[...], axis=-1, keepdims=True) * (1.0 / d_inner)
    ssq_ref[...] = jnp.broadcast_to(lax.rsqrt(ms + EPS), ssq_ref.shape)

    slab = 4 * SUBLANES

    def norm_body(s, carry):
        rr = pl.ds(pl.multiple_of(s * slab, slab), slab)
        inv_rms = jnp.tile(ssq_ref[rr, :], (1, d_inner // LANES))
        out_ref[rr, :] = (y_ref[rr, :] * inv_rms * ng_ref[...]).astype(BF16)
        return carry

    lax.fori_loop(0, tb // slab, norm_body, 0)


def _ssd_scan(proj, dt, a_log, d_skip, norm_g, *, batch, seq_len, d_inner, tb=256):
    m = proj.shape[0]
    bc_w = 2 * N_GROUPS * N_STATE
    n_l = seq_len // tb
    row = lambda b, l: b * n_l + l
    blocks = (2 * 2 * tb * d_inner * 4 + 2 * tb * bc_w * 4 + 2 * tb * LANES * 4 + 2 * tb * d_inner * 2
              + N_GROUPS * N_STATE * GROUP_WIDTH * 4 + tb * d_inner * 4 + tb * LANES * 4
              + 4 * N_GROUPS * HALF_WIDTH * HALF_WIDTH * 2)
    return pl.pallas_call(
        _ssd_kernel,
        grid=(batch, n_l),
        in_specs=[
            pl.BlockSpec((tb, d_inner), lambda b, l: (row(b, l), 0)),
            pl.BlockSpec((tb, d_inner), lambda b, l: (row(b, l), 1)),
            pl.BlockSpec((tb, bc_w), lambda b, l: (row(b, l), 2 * d_inner // bc_w)),
            pl.BlockSpec((tb, LANES), lambda b, l: (row(b, l), 0)),
            pl.BlockSpec((1, LANES), lambda b, l: (0, 0)),
            pl.BlockSpec((1, d_inner), lambda b, l: (0, 0)),
            pl.BlockSpec((1, d_inner), lambda b, l: (0, 0)),
        ],
        out_specs=pl.BlockSpec((tb, d_inner), lambda b, l: (row(b, l), 0)),
        out_shape=jax.ShapeDtypeStruct((m, d_inner), BF16),
        scratch_shapes=[
            pltpu.VMEM((N_GROUPS, N_STATE, GROUP_WIDTH), F32),
            pltpu.VMEM((tb, d_inner), F32),
            pltpu.VMEM((tb, LANES), F32),
            pltpu.VMEM((2 * N_GROUPS * 2, HALF_WIDTH, HALF_WIDTH), BF16),
        ],
        compiler_params=pltpu.CompilerParams(
            dimension_semantics=("arbitrary", "arbitrary"),
            vmem_limit_bytes=_vmem_limit(blocks)),
        name="ssd_scan",
    )(proj, proj, proj, dt, a_log, d_skip, norm_g)


def _outproj_kernel(a_ref, w_ref, x_ref, g_ref, *out_refs, final):
    acc = jnp.dot(a_ref[...], w_ref[...], preferred_element_type=F32)
    x1 = x_ref[...] + acc
    hn = _rmsnorm(x1, g_ref[...])
    if final:
        out_refs[0][...] = hn
    else:
        out_refs[0][...] = x1
        out_refs[1][...] = hn.astype(BF16)


def _outproj(a, w, x2, g, *, final, tm=512):
    m, k = a.shape
    d = w.shape[1]
    row_spec = pl.BlockSpec((tm, d), lambda i: (i, 0))
    if final:
        out_specs = [row_spec]
        out_shape = [jax.ShapeDtypeStruct((m, d), F32)]
    else:
        out_specs = [row_spec, row_spec]
        out_shape = [jax.ShapeDtypeStruct((m, d), F32), jax.ShapeDtypeStruct((m, d), BF16)]
    blocks = 2 * tm * k * 2 + k * d * 2 + 2 * tm * d * 4 + 2 * tm * d * 4 + 2 * tm * d * 2
    return pl.pallas_call(
        functools.partial(_outproj_kernel, final=final),
        grid=(m // tm,),
        in_specs=[
            pl.BlockSpec((tm, k), lambda i: (i, 0)),
            pl.BlockSpec((k, d), lambda i: (0, 0), pipeline_mode=pl.Buffered(1)),
            row_spec,
            pl.BlockSpec((1, d), lambda i: (0, 0)),
        ],
        out_specs=out_specs,
        out_shape=out_shape,
        compiler_params=pltpu.CompilerParams(
            dimension_semantics=("arbitrary",),
            vmem_limit_bytes=_vmem_limit(blocks)),
        name="outproj_final" if final else "outproj",
    )(a, w, x2, g)


def _pool_group_kernel(m_ref, w_ref, gate_ref, scale_ref, out_ref, wb_ref):
    @pl.when(pl.program_id(1) == 0)
    def _():
        wb_ref[...] = w_ref[...].astype(BF16)

    t = jnp.dot(m_ref[...], wb_ref[...], preferred_element_type=F32)
    out_ref[...] = (t * scale_ref[...] * _silu(gate_ref[...].astype(F32))).astype(out_ref.dtype)


def _pool_group(mixed, w_group, gate, scale, *, tm=1024):
    m, d_pool = mixed.shape
    n_groups, gd, _ = w_group.shape
    blocks = 3 * 2 * tm * gd * 2 + 2 * gd * gd * 4 + gd * gd * 2
    return pl.pallas_call(
        _pool_group_kernel,
        grid=(n_groups, m // tm),
        in_specs=[
            pl.BlockSpec((tm, gd), lambda g, i: (i, g)),
            pl.BlockSpec((None, gd, gd), lambda g, i: (g, 0, 0)),
            pl.BlockSpec((tm, gd), lambda g, i: (i, g)),
            pl.BlockSpec((1, gd), lambda g, i: (0, g)),
        ],
        out_specs=pl.BlockSpec((tm, gd), lambda g, i: (i, g)),
        out_shape=jax.ShapeDtypeStruct((m, d_pool), BF16),
        scratch_shapes=[pltpu.VMEM((gd, gd), BF16)],
        compiler_params=pltpu.CompilerParams(
            dimension_semantics=("arbitrary", "arbitrary"),
            vmem_limit_bytes=_vmem_limit(blocks)),
        name="pool_group",
    )(mixed, w_group, gate, scale)


def kernel(x, ln_g, final_g, ssm_w_in, ssm_conv_w, ssm_conv_b, ssm_dt_bias, ssm_a_log, ssm_d, ssm_norm_g, ssm_w_out, pool_w_in, pool_w_group, pool_scale, pool_w_out):
    batch, seq_len, d_model = x.shape
    m = batch * seq_len
    d_inner = ssm_w_out.shape[1]
    conv_dim = ssm_conv_w.shape[2]
    n_heads = ssm_dt_bias.shape[1]
    d_pool = pool_w_out.shape[1]
    assert d_inner == N_GROUPS * GROUP_WIDTH and conv_dim == d_inner + 2 * N_GROUPS * N_STATE
    assert n_heads == N_GROUPS * HEADS_PER_GROUP and n_heads <= LANES
    assert ssm_conv_w.shape[1] == CONV_TAPS and pool_w_group.shape[1] == len(POOL_WINDOWS)

    x2 = x.reshape(m, d_model)
    pad_heads = lambda v: jnp.pad(v, (0, LANES - n_heads)).reshape(1, LANES)

    w_in = ssm_w_in[0]
    n_main = d_inner + conv_dim
    w_dt = jnp.pad(w_in[:, n_main:], ((0, 0), (0, LANES - n_heads))).astype(BF16)
    proj, dt, w_out0, pw_in, w_out1 = _ssm_inproj(
        x2, ln_g[0].reshape(1, d_model), w_in.astype(BF16), w_dt,
        ssm_conv_w[0], ssm_conv_b[0].reshape(1, conv_dim), pad_heads(ssm_dt_bias[0]),
        (ssm_w_out[0], pool_w_in[0], pool_w_out[0]),
        seq_len=seq_len, d_inner=d_inner, conv_dim=conv_dim)
    d_skip = jnp.repeat(ssm_d[0], HEADDIM).reshape(1, d_inner)
    yn = _ssd_scan(proj, dt, pad_heads(ssm_a_log[0]), d_skip, ssm_norm_g[0].reshape(1, d_inner),
                   batch=batch, seq_len=seq_len, d_inner=d_inner)
    x1, hn1 = _outproj(yn, w_out0, x2, ln_g[1].reshape(1, d_model), final=False)

    mixed, gate = _pool_inproj(hn1, pw_in, seq_len=seq_len, d_pool=d_pool)
    y1 = _pool_group(mixed, pool_w_group[0], gate, pool_scale[0].reshape(1, d_pool))
    (out,) = _outproj(y1, w_out1, x1, final_g.reshape(1, d_model), final=True)
    return out.reshape(batch, seq_len, d_model)
```
